```python
import math
import jax
import jax.numpy as jnp
from jax import lax
import numpy as np

D_MODEL = 1024
BATCH = 1
SEQ = 16384
DEPTH = 2
DEC_BATCH = 32
DEC_SEQ = 1
PAST_LEN = 16384
PAGE_SIZE = 128

HEAD_DIM = 64
H_SB = D_MODEL // (2 * HEAD_DIM)
H_MB = D_MODEL // (2 * HEAD_DIM)
W_SB = H_SB * HEAD_DIM
W_MB = H_MB * HEAD_DIM
MIX_WIDTH = W_SB + W_MB
MOBA_BLOCK = 256
MOBA_TOPK = 3
Q_BLOCK = 128
ROPE_THETA = 10000.0
D_FF = 128 * ((8 * D_MODEL // 3 + 127) // 128)
CONV_W = 3
RMS_EPS = 1e-6
N_PAGES = PAST_LEN // PAGE_SIZE

kernel_name = "hymba_stickbreak_moba_convffn_step"


def rmsnorm(x, g):
    xf = x.astype(jnp.float32)
    y = xf * lax.rsqrt(jnp.mean(xf * xf, axis=-1, keepdims=True) + RMS_EPS)
    return (y * g.astype(jnp.float32)).astype(x.dtype)


def rope(x, pos):
    half = HEAD_DIM // 2
    inv_freq = ROPE_THETA ** (-jnp.arange(half, dtype=jnp.float32) / half)
    ang = pos.astype(jnp.float32)[:, None] * inv_freq[None, :]
    cos = jnp.cos(ang)[None, :, None, :]
    sin = jnp.sin(ang)[None, :, None, :]
    xf = x.astype(jnp.float32)
    x1, x2 = xf[..., :half], xf[..., half:]
    return jnp.concatenate([x1 * cos - x2 * sin, x2 * cos + x1 * sin], axis=-1).astype(x.dtype)


def stick_breaking_attend(q, q_pos, k, v):
    L = k.shape[1]
    z = jnp.einsum('bqhd,bkhd->bhqk', q, k).astype(jnp.float32) * (HEAD_DIM ** -0.5)
    causal = jnp.arange(L)[None, :] < q_pos[:, None]
    log_keep = jnp.where(causal, jax.nn.log_sigmoid(-z), 0.0)
    between = lax.cumsum(log_keep, axis=3, reverse=True) - log_keep
    w = jnp.where(causal, jnp.exp(jax.nn.log_sigmoid(z) + between), 0.0)
    return jnp.einsum('bhqk,bkhd->bqhd', w.astype(v.dtype), v)


def moba_blocks(k, v):
    B, L, H, Dh = k.shape
    nb = -(-L // MOBA_BLOCK)
    padw = ((0, 0), (0, nb * MOBA_BLOCK - L), (0, 0), (0, 0))
    kb = jnp.pad(k, padw).reshape(B, nb, MOBA_BLOCK, H, Dh).transpose(0, 3, 1, 2, 4)
    vb = jnp.pad(v, padw).reshape(B, nb, MOBA_BLOCK, H, Dh).transpose(0, 3, 1, 2, 4)
    kmean = jnp.mean(kb.astype(jnp.float32), axis=3)
    return kb, vb, kmean


def moba_attend(q, q_pos, kb, vb, kmean):
    B, Q, H, Dh = q.shape
    nb = kb.shape[2]
    q_blk = q_pos // MOBA_BLOCK
    gate = jnp.einsum('bqhd,bhnd->bqhn', q.astype(jnp.float32), kmean)
    fully_past = jnp.arange(nb)[None, :] < q_blk[:, None]
    gate = jnp.where(fully_past[None, :, None, :], gate, -jnp.inf)
    _, top = lax.top_k(gate, min(MOBA_TOPK, nb))
    own = jnp.broadcast_to(q_blk[None, :, None, None], (B, Q, H, 1)).astype(top.dtype)
    blocks = jnp.concatenate([top, own], axis=-1)
    bi = jnp.arange(B)[:, None, None, None]
    hi = jnp.arange(H)[None, None, :, None]
    kg = kb[bi, hi, blocks]
    vg = vb[bi, hi, blocks]
    s = jnp.einsum('bqhd,bqhnkd->bqhnk', q, kg).astype(jnp.float32) * (HEAD_DIM ** -0.5)
    key_pos = blocks[..., None] * MOBA_BLOCK + jnp.arange(MOBA_BLOCK)
    sel_ok = jnp.concatenate([top < q_blk[None, :, None, None],
                              jnp.ones((B, Q, H, 1), dtype=bool)], axis=-1)
    ok = sel_ok[..., None] & (key_pos <= q_pos[None, :, None, None, None])
    s = jnp.where(ok, s, -jnp.inf)
    p = jax.nn.softmax(s.reshape(B, Q, H, -1), axis=-1).reshape(s.shape)
    return jnp.einsum('bqhnk,bqhnkd->bqhd', p.astype(vg.dtype), vg)


def over_query_blocks(fn, q, q_pos):
    B, Q, H, Dh = q.shape
    if Q > Q_BLOCK and Q % Q_BLOCK == 0:
        nq = Q // Q_BLOCK
        qb = q.reshape(B, nq, Q_BLOCK, H, Dh).transpose(1, 0, 2, 3, 4)
        pb = q_pos.reshape(nq, Q_BLOCK)
        out = lax.map(lambda a: fn(a[0], a[1]), (qb, pb))
        return out.transpose(1, 0, 2, 3, 4).reshape(B, Q, H, Dh)
    return fn(q, q_pos)


def token_mixers(h, pos, past, w_in, g_sb, g_mb, w_out):
    B, T, _ = h.shape
    proj = h @ w_in
    offs = [W_SB, 2 * W_SB, 3 * W_SB, 3 * W_SB + W_MB, 3 * W_SB + 2 * W_MB]
    q_sb, k_sb, v_sb, q_mb, k_mb, v_mb = jnp.split(proj, offs, axis=-1)
    q_sb = q_sb.reshape(B, T, H_SB, HEAD_DIM)
    k_sb = k_sb.reshape(B, T, H_SB, HEAD_DIM)
    v_sb = v_sb.reshape(B, T, H_SB, HEAD_DIM)
    q_mb = rope(q_mb.reshape(B, T, H_MB, HEAD_DIM), pos)
    k_mb = rope(k_mb.reshape(B, T, H_MB, HEAD_DIM), pos)
    v_mb = v_mb.reshape(B, T, H_MB, HEAD_DIM)
    if past is None:
        ks_all, vs_all, km_all, vm_all = k_sb, v_sb, k_mb, v_mb
    else:
        pk_sb, pv_sb, pk_mb, pv_mb = past
        ks_all = jnp.concatenate([pk_sb.astype(k_sb.dtype), k_sb], axis=1)
        vs_all = jnp.concatenate([pv_sb.astype(v_sb.dtype), v_sb], axis=1)
        km_all = jnp.concatenate([pk_mb.astype(k_mb.dtype), k_mb], axis=1)
        vm_all = jnp.concatenate([pv_mb.astype(v_mb.dtype), v_mb], axis=1)
    o_sb = over_query_blocks(lambda qq, pp: stick_breaking_attend(qq, pp, ks_all, vs_all), q_sb, pos)
    kb, vb, kmean = moba_blocks(km_all, vm_all)
    o_mb = over_query_blocks(lambda qq, pp: moba_attend(qq, pp, kb, vb, kmean), q_mb, pos)
    o = jnp.concatenate([rmsnorm(o_sb, g_sb.reshape(H_SB, HEAD_DIM)).reshape(B, T, W_SB),
                         rmsnorm(o_mb, g_mb.reshape(H_MB, HEAD_DIM)).reshape(B, T, W_MB)], axis=-1)
    return o @ w_out, (k_sb, v_sb, k_mb, v_mb)


def conv_ffn(h, conv_prev, w_up, conv_w, conv_b, w_down):
    T = h.shape[1]
    u = h @ w_up
    ext = jnp.concatenate([conv_prev.astype(u.dtype), u], axis=1)
    conv = conv_b + sum(ext[:, i:i + T] * conv_w[i] for i in range(CONV_W))
    val, gate = jnp.split(conv, 2, axis=-1)
    return (val * jax.nn.silu(gate)) @ w_down, ext[:, T:]


def decoder_layer(x, c, pos, past, conv_prev, w_ada, b_ada, g_attn, w_in, g_sb, g_mb, w_out,
                  g_ffn, w_up, conv_w, conv_b, w_down):
    mod = (jax.nn.silu(c) @ w_ada + b_ada)[:, None, :]
    sh1, sc1, ga1, sh2, sc2, ga2 = jnp.split(mod, 6, axis=-1)
    h = rmsnorm(x, g_attn) * (1 + sc1) + sh1
    mix, rows = token_mixers(h, pos, past, w_in, g_sb, g_mb, w_out)
    x = x + ga1 * mix
    h = rmsnorm(x, g_ffn) * (1 + sc2) + sh2
    ffn, conv_state = conv_ffn(h, conv_prev, w_up, conv_w, conv_b, w_down)
    x = x + ga2 * ffn
    return x, rows, conv_state


def gather_pages(cache_l, page_table):
    p = cache_l[page_table]
    return p.reshape(p.shape[0], p.shape[1] * p.shape[2], p.shape[3], p.shape[4])


def setup_inputs(seed: int = 0) -> dict:
    key = jax.random.key(seed)
    ks = jax.random.split(key, 24)
    f32 = jnp.float32
    n_pages = PAST_LEN // PAGE_SIZE
    n_used = DEC_BATCH * n_pages
    n_pool = n_used + max(1, n_used // 4)

    def nrm(k, shape, s=1.0):
        return jax.random.normal(k, shape, f32) * s

    page_table = jax.random.permutation(ks[9], n_pool)[:n_used].reshape(DEC_BATCH, n_pages).astype(jnp.int32)
    return {
        "x_prompt": nrm(ks[0], (BATCH, SEQ, D_MODEL)),
        "x_sample": nrm(ks[1], (DEC_BATCH, DEC_SEQ, D_MODEL)),
        "c_prompt": nrm(ks[2], (BATCH, D_MODEL)),
        "c_sample": nrm(ks[3], (DEC_BATCH, D_MODEL)),
        "cache_sb_k": nrm(ks[4], (DEPTH, n_pool, PAGE_SIZE, H_SB, HEAD_DIM)),
        "cache_sb_v": nrm(ks[5], (DEPTH, n_pool, PAGE_SIZE, H_SB, HEAD_DIM)),
        "cache_moba_k": nrm(ks[6], (DEPTH, n_pool, PAGE_SIZE, H_MB, HEAD_DIM)),
        "cache_moba_v": nrm(ks[7], (DEPTH, n_pool, PAGE_SIZE, H_MB, HEAD_DIM)),
        "state_ffn_conv": nrm(ks[8], (DEPTH, DEC_BATCH, CONV_W - 1, 2 * D_FF)),
        "page_table": page_table,
        "w_ada": nrm(ks[10], (DEPTH, D_MODEL, 6 * D_MODEL), 0.5 * D_MODEL ** -0.5),
        "b_ada": nrm(ks[11], (DEPTH, 6 * D_MODEL), 0.02),
        "g_attn": 1.0 + nrm(ks[12], (DEPTH, D_MODEL), 0.02),
        "w_in": nrm(ks[13], (DEPTH, D_MODEL, 3 * MIX_WIDTH), D_MODEL ** -0.5),
        "g_sb_out": 1.0 + nrm(ks[14], (DEPTH, W_SB), 0.02),
        "g_moba_out": 1.0 + nrm(ks[15], (DEPTH, W_MB), 0.02),
        "w_out": nrm(ks[16], (DEPTH, MIX_WIDTH, D_MODEL), MIX_WIDTH ** -0.5),
        "g_ffn": 1.0 + nrm(ks[17], (DEPTH, D_MODEL), 0.02),
        "w_up": nrm(ks[18], (DEPTH, D_MODEL, 2 * D_FF), D_MODEL ** -0.5),
        "conv_w": nrm(ks[19], (DEPTH, CONV_W, 2 * D_FF), CONV_W ** -0.5),
        "conv_b": nrm(ks[20], (DEPTH, 2 * D_FF), 0.02),
        "w_down": nrm(ks[21], (DEPTH, D_FF, D_MODEL), D_FF ** -0.5),
        "g_final": 1.0 + nrm(ks[22], (D_MODEL,), 0.02),
    }


def reference(x_prompt, x_sample, c_prompt, c_sample, cache_sb_k, cache_sb_v, cache_moba_k,
              cache_moba_v, state_ffn_conv, page_table, w_ada, b_ada, g_attn, w_in, g_sb_out,
              g_moba_out, w_out, g_ffn, w_up, conv_w, conv_b, w_down, g_final):
    past_len = page_table.shape[1] * cache_sb_k.shape[2]
    pos_p = jnp.arange(x_prompt.shape[1])
    pos_s = past_len + jnp.arange(x_sample.shape[1])
    xp, xs = x_prompt, x_sample
    rows_p, rows_s, conv_p, conv_s = [], [], [], []
    for l in range(DEPTH):
        lw = (w_ada[l], b_ada[l], g_attn[l], w_in[l], g_sb_out[l], g_moba_out[l], w_out[l],
              g_ffn[l], w_up[l], conv_w[l], conv_b[l], w_down[l])
        conv0 = jnp.zeros((xp.shape[0], CONV_W - 1, 2 * D_FF), xp.dtype)
        xp, rp, cp = decoder_layer(xp, c_prompt, pos_p, None, conv0, *lw)
        past = (gather_pages(cache_sb_k[l], page_table), gather_pages(cache_sb_v[l], page_table),
                gather_pages(cache_moba_k[l], page_table), gather_pages(cache_moba_v[l], page_table))
        xs, rs, cs = decoder_layer(xs, c_sample, pos_s, past, state_ffn_conv[l], *lw)
        rows_p.append(rp)
        rows_s.append(rs)
        conv_p.append(cp)
        conv_s.append(cs)
    y_prompt = rmsnorm(xp, g_final)
    y_sample = rmsnorm(xs, g_final)
    return (y_prompt, y_sample,
            jnp.stack([r[0] for r in rows_p]), jnp.stack([r[1] for r in rows_p]),
            jnp.stack([r[2] for r in rows_p]), jnp.stack([r[3] for r in rows_p]),
            jnp.stack(conv_p),
            jnp.stack([r[0] for r in rows_s]), jnp.stack([r[1] for r in rows_s]),
            jnp.stack([r[2] for r in rows_s]), jnp.stack([r[3] for r in rows_s]),
            jnp.stack(conv_s))
```

```python
import functools

import jax
import jax.numpy as jnp
from jax import lax
from jax.experimental import pallas as pl
from jax.experimental.pallas import tpu as pltpu

HEAD_DIM = 64
MOBA_BLOCK = 256
MOBA_TOPK = 3
ROPE_THETA = 10000.0
CONV_W = 3
RMS_EPS = 1e-6

LANES = 128
SUBLANES = 8
VMEM_LIMIT_BYTES = 56 * 1024 * 1024
NEG_BIG = -1e30

F32 = jnp.float32
BF16 = jnp.bfloat16
NT_DIMS = (((1,), (1,)), ((), ()))


def _params(*sem):
    return pltpu.CompilerParams(dimension_semantics=sem, vmem_limit_bytes=VMEM_LIMIT_BYTES)


def _dot(a, b):
    return jnp.dot(a, b, preferred_element_type=F32)


def _dot_nt(a, b):
    return lax.dot_general(a, b, NT_DIMS, preferred_element_type=F32)


def _split(x):
    hi = x.astype(BF16)
    lo = (x - hi.astype(F32)).astype(BF16)
    return hi, lo


def _rms(x):
    return x * lax.rsqrt(jnp.mean(x * x, axis=-1, keepdims=True) + RMS_EPS)


def _softplus(z):
    return jnp.maximum(z, 0.0) + jnp.log(1.0 + jnp.exp(-jnp.abs(z)))


def _row_spec(rows, tm, width):
    if rows == 1:
        return pl.BlockSpec((1, width), lambda i: (0, 0))
    return pl.BlockSpec((tm, width), lambda i: (i, 0))


def _const_spec(shape):
    return pl.BlockSpec(shape, lambda *_: (0,) * len(shape), pipeline_mode=pl.Buffered(1))


def _ada_kernel(c_ref, w_ref, b_ref, o_ref):
    c = c_ref[...]
    s = c * (1.0 / (1.0 + jnp.exp(-c)))
    o_ref[0] = jnp.dot(s, w_ref[0], preferred_element_type=F32,
                       precision=lax.Precision.HIGHEST) + b_ref[0]


def _ada(c_all, w_ada, b_ada):
    depth, d, n = w_ada.shape
    r = c_all.shape[0]
    tn = 1536 if n % 1536 == 0 else n
    return pl.pallas_call(
        _ada_kernel,
        grid=(depth, n // tn),
        in_specs=[pl.BlockSpec((r, d), lambda l, j: (0, 0)),
                  pl.BlockSpec((1, d, tn), lambda l, j: (l, 0, j)),
                  pl.BlockSpec((1, 1, tn), lambda l, j: (l, 0, j))],
        out_specs=pl.BlockSpec((1, r, tn), lambda l, j: (l, 0, j)),
        out_shape=jax.ShapeDtypeStruct((depth, r, n), F32),
        compiler_params=_params("arbitrary", "arbitrary"),
        name="ada_mod",
    )(c_all, w_ada, b_ada.reshape(depth, 1, n))


def _rope(x, cos, sin_a, sin_b):
    outs = []
    for c in range(x.shape[1] // LANES):
        xc = x[:, c * LANES:(c + 1) * LANES]
        outs.append(xc * cos + pltpu.roll(xc, LANES - HEAD_DIM // 2, 1) * sin_a
                    + pltpu.roll(xc, HEAD_DIM // 2, 1) * sin_b)
    return jnp.concatenate(outs, axis=1)


def _proj_kernel(x_ref, sc_ref, sh_ref, g_ref, wsb_ref, wqk_hi_ref, wqk_lo_ref, wmv_ref,
                 cos_ref, sa_ref, sb_ref,
                 ksb_ref, vsb_ref, kmb_ref, vmb_ref, qmbf_ref,
                 qsb16_ref, ksb16_ref, vsb16_ref, qmb16_ref, kmb16_ref, vmb16_ref, *, w):
    h = _rms(x_ref[...]) * g_ref[...] * (1.0 + sc_ref[...]) + sh_ref[...]
    h_hi, h_lo = _split(h)
    scale = HEAD_DIM ** -0.5

    sb = _dot(h_hi, wsb_ref[...])
    q_sb, k_sb, v_sb = sb[:, :w], sb[:, w:2 * w], sb[:, 2 * w:]
    ksb_ref[...] = k_sb
    vsb_ref[...] = v_sb
    qsb16_ref[...] = (q_sb * scale).astype(BF16)
    ksb16_ref[...] = k_sb.astype(BF16)
    vsb16_ref[...] = v_sb.astype(BF16)

    w_hi = wqk_hi_ref[...]
    qk = _dot(h_hi, w_hi) + _dot(h_hi, wqk_lo_ref[...]) + _dot(h_lo, w_hi)
    cos, sa, sb_t = cos_ref[...], sa_ref[...], sb_ref[...]
    q_mb = _rope(qk[:, :w], cos, sa, sb_t)
    k_mb = _rope(qk[:, w:], cos, sa, sb_t)
    v_mb = _dot(h_hi, wmv_ref[...])
    kmb_ref[...] = k_mb
    vmb_ref[...] = v_mb
    qmbf_ref[...] = q_mb
    qmb16_ref[...] = (q_mb * scale).astype(BF16)
    kmb16_ref[...] = k_mb.astype(BF16)
    vmb16_ref[...] = v_mb.astype(BF16)


def _proj(x, sc, sh, g, wts, rope_tabs, tm):
    t, d = x.shape
    wsb, wqk_hi, wqk_lo, wmv = wts
    w = wmv.shape[1]
    cos, sa, sb = rope_tabs
    row = lambda width: pl.BlockSpec((tm, width), lambda i: (i, 0))
    f32_out = jax.ShapeDtypeStruct((t, w), F32)
    b16_out = jax.ShapeDtypeStruct((t, w), BF16)
    return pl.pallas_call(
        functools.partial(_proj_kernel, w=w),
        grid=(t // tm,),
        in_specs=[row(d), _row_spec(sc.shape[0], tm, d), _row_spec(sh.shape[0], tm, d),
                  _const_spec((1, d)), _const_spec(wsb.shape), _const_spec(wqk_hi.shape),
                  _const_spec(wqk_lo.shape), _const_spec(wmv.shape),
                  row(LANES), row(LANES), row(LANES)],
        out_specs=[row(w)] * 11,
        out_shape=[f32_out] * 5 + [b16_out] * 6,
        compiler_params=_params("arbitrary"),
        name="qkv_proj",
    )(x, sc, sh, g, wsb, wqk_hi, wqk_lo, wmv, cos, sa, sb)


def _head_norm_pair(o, first, g):
    sq = o * o
    ms_a = jnp.sum(jnp.where(first, sq, 0.0), axis=1, keepdims=True) * (1.0 / HEAD_DIM)
    ms_b = jnp.sum(jnp.where(first, 0.0, sq), axis=1, keepdims=True) * (1.0 / HEAD_DIM)
    r = jnp.where(first, lax.rsqrt(ms_a + RMS_EPS), lax.rsqrt(ms_b + RMS_EPS))
    return o * r * g


def _sb_prompt_kernel(q_ref, k_ref, v_ref, g_ref, o_ref, *, tq):
    i = pl.program_id(1)
    q = q_ref[...]
    lane = lax.broadcasted_iota(jnp.int32, (tq, LANES), 1)
    first = lane < HEAD_DIM
    zero = jnp.zeros_like(q)
    qs = (jnp.where(first, q, zero), jnp.where(first, zero, q))
    r_i = lax.broadcasted_iota(jnp.int32, (tq, tq), 0)
    c_i = lax.broadcasted_iota(jnp.int32, (tq, tq), 1)
    neg_upper = jnp.where(r_i > c_i, -1.0, 0.0).astype(BF16)
    causal = c_i < r_i

    def tile(j, state, diag):
        start = pl.multiple_of(j * tq, tq)
        kt = k_ref[pl.ds(start, tq), :]
        vt = v_ref[pl.ds(start, tq), :]
        out = []
        for a in range(2):
            keep_sum, acc = state[2 * a], state[2 * a + 1]
            z = _dot_nt(qs[a], kt)
            sp = _softplus(z)
            spm = jnp.where(causal, sp, 0.0) if diag else sp
            hi, lo = _split(spm)
            between = _dot(hi, neg_upper) + _dot(lo, neg_upper)
            wgt = jnp.exp(z - sp + between - keep_sum)
            if diag:
                wgt = jnp.where(causal, wgt, 0.0)
            acc = acc + _dot(wgt.astype(BF16), vt)
            keep_sum = keep_sum + jnp.sum(spm, axis=1, keepdims=True)
            out += [keep_sum, acc]
        return tuple(out)

    zc = jnp.zeros((tq, 1), F32)
    za = jnp.zeros((tq, LANES), F32)
    state = tile(i, (zc, za, zc, za), True)
    state = lax.fori_loop(0, i, lambda jj, st: tile(i - 1 - jj, st, False), state)
    o = jnp.where(first, state[1], state[3])
    o_ref[...] = _head_norm_pair(o, first, g_ref[...]).astype(o_ref.dtype)


def _sb_prompt(q16, k16, v16, g, tq):
    t, w = q16.shape
    return pl.pallas_call(
        functools.partial(_sb_prompt_kernel, tq=tq),
        grid=(w // LANES, t // tq),
        in_specs=[pl.BlockSpec((tq, LANES), lambda p, i: (i, p)),
                  pl.BlockSpec((t, LANES), lambda p, i: (0, p)),
                  pl.BlockSpec((t, LANES), lambda p, i: (0, p)),
                  pl.BlockSpec((1, LANES), lambda p, i: (0, p))],
        out_specs=pl.BlockSpec((tq, LANES), lambda p, i: (i, p)),
        out_shape=jax.ShapeDtypeStruct((t, w), BF16),
        compiler_params=_params("arbitrary", "arbitrary"),
        name="sb_prompt_attn",
    )(q16, k16, v16, g)


def _kmean_kernel(k_ref, o_ref):
    o_ref[0] = jnp.mean(k_ref[...], axis=0, keepdims=True)


def _kmean(k_mb):
    t, w = k_mb.shape
    nb = t // MOBA_BLOCK
    return pl.pallas_call(
        _kmean_kernel,
        grid=(nb,),
        in_specs=[pl.BlockSpec((MOBA_BLOCK, w), lambda n: (n, 0))],
        out_specs=pl.BlockSpec((1, 1, w), lambda n: (n, 0, 0)),
        out_shape=jax.ShapeDtypeStruct((nb, 1, w), F32),
        compiler_params=_params("arbitrary"),
        name="moba_kmean",
    )(k_mb).reshape(nb, w)


def _moba_prompt_kernel(qf_ref, q_ref, k_ref, v_ref, km_ref, g_ref, o_ref, *, tq, nbp):
    i = pl.program_id(1)
    q = q_ref[...]
    qf = qf_ref[...]
    km_hi, km_lo = _split(km_ref[...])
    lane = lax.broadcasted_iota(jnp.int32, (tq, LANES), 1)
    first = lane < HEAD_DIM
    col = lax.broadcasted_iota(jnp.int32, (tq, nbp), 1)
    past = col < i
    r_i = lax.broadcasted_iota(jnp.int32, (tq, tq), 0)
    c_i = lax.broadcasted_iota(jnp.int32, (tq, tq), 1)
    own_ok = c_i <= r_i
    blk_row = lax.broadcasted_iota(jnp.int32, (nbp, tq), 0)

    own_start = pl.multiple_of(i * tq, tq)
    k_own = k_ref[pl.ds(own_start, tq), :]
    v_own = v_ref[pl.ds(own_start, tq), :]

    qs, biases, state = [], [], []
    for a in range(2):
        mask = first if a == 0 else jnp.logical_not(first)
        qa = jnp.where(mask, q, jnp.zeros_like(q))
        qs.append(qa)
        qf_hi, qf_lo = _split(jnp.where(mask, qf, 0.0))
        gate = _dot_nt(qf_hi, km_hi) + _dot_nt(qf_hi, km_lo) + _dot_nt(qf_lo, km_hi)
        gate = jnp.where(past, gate, -jnp.inf)
        bias = jnp.full((tq, nbp), NEG_BIG, F32)
        for _ in range(MOBA_TOPK):
            best = jnp.max(gate, axis=1, keepdims=True)
            idx = jnp.min(jnp.where(gate == best, col, nbp), axis=1, keepdims=True)
            hit = col == idx
            bias = jnp.where(jnp.logical_and(hit, past), 0.0, bias)
            gate = jnp.where(hit, -jnp.inf, gate)
        biases.append(bias.astype(BF16))
        s = jnp.where(own_ok, _dot_nt(qa, k_own), -jnp.inf)
        m = jnp.max(s, axis=1, keepdims=True)
        p = jnp.exp(s - m)
        state += [m, jnp.sum(p, axis=1, keepdims=True), _dot(p.astype(BF16), v_own)]

    def body(j, st):
        start = pl.multiple_of(j * tq, tq)
        kt = k_ref[pl.ds(start, tq), :]
        vt = v_ref[pl.ds(start, tq), :]
        onehot = jnp.where(blk_row == j, 1.0, 0.0).astype(BF16)
        out = []
        for a in range(2):
            m, l, acc = st[3 * a], st[3 * a + 1], st[3 * a + 2]
            s = _dot_nt(qs[a], kt) + _dot(biases[a], onehot)
            m_new = jnp.maximum(m, jnp.max(s, axis=1, keepdims=True))
            alpha = jnp.exp(m - m_new)
            p = jnp.exp(s - m_new)
            l = alpha * l + jnp.sum(p, axis=1, keepdims=True)
            acc = alpha * acc + _dot(p.astype(BF16), vt)
            out += [m_new, l, acc]
        return tuple(out)

    st = lax.fori_loop(0, i, body, tuple(state))
    o = jnp.where(first, st[2] / st[1], st[5] / st[4])
    o_ref[...] = _head_norm_pair(o, first, g_ref[...]).astype(o_ref.dtype)


def _moba_prompt(qf, q16, k16, v16, kmean_p, g):
    t, w = q16.shape
    tq = MOBA_BLOCK
    nbp = kmean_p.shape[0]
    return pl.pallas_call(
        functools.partial(_moba_prompt_kernel, tq=tq, nbp=nbp),
        grid=(w // LANES, t // tq),
        in_specs=[pl.BlockSpec((tq, LANES), lambda p, i: (i, p)),
                  pl.BlockSpec((tq, LANES), lambda p, i: (i, p)),
                  pl.BlockSpec((t, LANES), lambda p, i: (0, p)),
                  pl.BlockSpec((t, LANES), lambda p, i: (0, p)),
                  pl.BlockSpec((nbp, LANES), lambda p, i: (0, p)),
                  pl.BlockSpec((1, LANES), lambda p, i: (0, p))],
        out_specs=pl.BlockSpec((tq, LANES), lambda p, i: (i, p)),
        out_shape=jax.ShapeDtypeStruct((t, w), BF16),
        compiler_params=_params("arbitrary", "arbitrary"),
        name="moba_prompt_attn",
    )(qf, q16, k16, v16, kmean_p, g)


def _outproj_kernel(osb_ref, omb_ref, wa_ref, wb_ref, x_ref, ga_ref, g_ref, sc_ref, sh_ref,
                    xo_ref, h_ref):
    mix = _dot(osb_ref[...], wa_ref[...]) + _dot(omb_ref[...], wb_ref[...])
    x = x_ref[...] + ga_ref[...] * mix
    xo_ref[...] = x
    h_ref[...] = (_rms(x) * g_ref[...] * (1.0 + sc_ref[...]) + sh_ref[...]).astype(h_ref.dtype)


def _outproj(o_sb, o_mb, w_a, w_b, x, ga, g, sc, sh, tm):
    t, d = x.shape
    w = o_sb.shape[1]
    row = lambda width: pl.BlockSpec((tm, width), lambda i: (i, 0))
    mod = lambda a: _row_spec(a.shape[0], tm, d)
    return pl.pallas_call(
        _outproj_kernel,
        grid=(t // tm,),
        in_specs=[row(w), row(w), _const_spec(w_a.shape), _const_spec(w_b.shape), row(d),
                  mod(ga), _const_spec((1, d)), mod(sc), mod(sh)],
        out_specs=[row(d), row(d)],
        out_shape=[jax.ShapeDtypeStruct((t, d), F32), jax.ShapeDtypeStruct((t, d), BF16)],
        compiler_params=_params("arbitrary"),
        name="out_proj",
    )(o_sb, o_mb, w_a, w_b, x, ga, g, sc, sh)


FFN_CHUNK = 256


def _silu_gate(val, gate):
    return val * gate * (1.0 / (1.0 + jnp.exp(-gate)))


def _ffn_prompt_kernel(h_ref, wup_ref, cw_ref, cb_ref, wdn_ref, x_ref, ga_ref, prev_ref,
                       xo_ref, state_ref, carry_ref, *, d_ff, tm):
    i = pl.program_id(0)

    @pl.when(i == 0)
    def _():
        carry_ref[...] = prev_ref[...]

    h = h_ref[...]
    row = lax.broadcasted_iota(jnp.int32, (tm, FFN_CHUNK), 0)

    def conv(off):
        cols = pl.ds(off, FFN_CHUNK)
        u = _dot(h, wup_ref[:, cols])
        prev = carry_ref[:, cols]
        p1, p2 = prev[SUBLANES - 1:SUBLANES, :], prev[SUBLANES - 2:SUBLANES - 1, :]
        u1 = jnp.where(row == 0, p1, pltpu.roll(u, 1, 0))
        u2 = jnp.where(row == 0, p2, jnp.where(row == 1, p1, pltpu.roll(u, 2, 0)))
        carry_ref[:, cols] = u[tm - SUBLANES:, :]
        cw = cw_ref[:, cols]
        return cb_ref[:, cols] + u2 * cw[0:1, :] + u1 * cw[1:2, :] + u * cw[2:3, :]

    acc = jnp.zeros((tm, xo_ref.shape[1]), F32)
    for c in range(d_ff // FFN_CHUNK):
        off = c * FFN_CHUNK
        act = _silu_gate(conv(off), conv(d_ff + off))
        acc = acc + _dot(act.astype(BF16), wdn_ref[pl.ds(off, FFN_CHUNK), :])
    xo_ref[...] = x_ref[...] + ga_ref[...] * acc

    @pl.when(i == pl.num_programs(0) - 1)
    def _():
        state_ref[...] = carry_ref[...]


def _ffn_prompt(h, w_up, conv_w, conv_b, w_down, x, ga, prev8, tm):
    t, d = x.shape
    d_ff = w_down.shape[0]
    row = lambda width: pl.BlockSpec((tm, width), lambda i: (i, 0))
    return pl.pallas_call(
        functools.partial(_ffn_prompt_kernel, d_ff=d_ff, tm=tm),
        grid=(t // tm,),
        in_specs=[row(d), _const_spec(w_up.shape), _const_spec(conv_w.shape),
                  _const_spec(conv_b.shape), _const_spec(w_down.shape), row(d),
                  _row_spec(ga.shape[0], tm, d), _const_spec(prev8.shape)],
        out_specs=[row(d), pl.BlockSpec(prev8.shape, lambda i: (0, 0))],
        out_shape=[jax.ShapeDtypeStruct((t, d), F32), jax.ShapeDtypeStruct(prev8.shape, F32)],
        scratch_shapes=[pltpu.VMEM(prev8.shape, F32)],
        compiler_params=_params("arbitrary"),
        name="conv_ffn_prompt",
    )(h, w_up, conv_w, conv_b, w_down, x, ga, prev8)


def _ffn_decode_kernel(h_ref, wup_ref, cw_ref, cb_ref, wdn_ref, x_ref, ga_ref, p0_ref, p1_ref,
                       xo_ref, u_ref, *, d_ff):
    h = h_ref[...]

    def conv(off):
        cols = pl.ds(off, FFN_CHUNK)
        u = _dot(h, wup_ref[:, cols])
        u_ref[:, cols] = u
        cw = cw_ref[:, cols]
        return (cb_ref[:, cols] + p0_ref[:, cols] * cw[0:1, :] + p1_ref[:, cols] * cw[1:2, :]
                + u * cw[2:3, :])

    acc = jnp.zeros(xo_ref.shape, F32)
    for c in range(d_ff // FFN_CHUNK):
        off = c * FFN_CHUNK
        act = _silu_gate(conv(off), conv(d_ff + off))
        acc = acc + _dot(act.astype(BF16), wdn_ref[pl.ds(off, FFN_CHUNK), :])
    xo_ref[...] = x_ref[...] + ga_ref[...] * acc


def _ffn_decode(h, w_up, conv_w, conv_b, w_down, x, ga, prev0, prev1):
    b, d = x.shape
    d_ff = w_down.shape[0]
    full = lambda a: pl.BlockSpec(a.shape, lambda i: (0,) * a.ndim)
    args = (h, w_up, conv_w, conv_b, w_down, x, ga, prev0, prev1)
    return pl.pallas_call(
        functools.partial(_ffn_decode_kernel, d_ff=d_ff),
        grid=(1,),
        in_specs=[full(a) for a in args],
        out_specs=[pl.BlockSpec((b, d), lambda i: (0, 0)),
                   pl.BlockSpec((b, 2 * d_ff), lambda i: (0, 0))],
        out_shape=[jax.ShapeDtypeStruct((b, d), F32), jax.ShapeDtypeStruct((b, 2 * d_ff), F32)],
        compiler_params=_params("arbitrary"),
        name="conv_ffn_decode",
    )(*args)


def _final_norm_kernel(x_ref, g_ref, o_ref):
    o_ref[...] = _rms(x_ref[...]) * g_ref[...]


def _final_norm(x, g, tm):
    t, d = x.shape
    return pl.pallas_call(
        _final_norm_kernel,
        grid=(t // tm,),
        in_specs=[pl.BlockSpec((tm, d), lambda i: (i, 0)), pl.BlockSpec((1, d), lambda i: (0, 0))],
        out_specs=pl.BlockSpec((tm, d), lambda i: (i, 0)),
        out_shape=jax.ShapeDtypeStruct((t, d), F32),
        compiler_params=_params("arbitrary"),
        name="final_norm",
    )(x, g)


def _page_lanes(page, heads):
    shape = (heads, page * heads)
    row = lax.broadcasted_iota(jnp.int32, shape, 0)
    lane = lax.broadcasted_iota(jnp.int32, shape, 1)
    return row, lane, (lane % heads) == row


def _sb_decode_kernel(pt_ref, q_ref, k_ref, v_ref, g_ref, o_ref, keep_ref, acc_ref, *, page, heads):
    del pt_ref
    p = pl.program_id(1)

    @pl.when(p == 0)
    def _():
        keep_ref[...] = jnp.zeros_like(keep_ref)
        acc_ref[...] = jnp.zeros_like(acc_ref)

    n = page * heads
    q = (q_ref[0] * HEAD_DIM ** -0.5).astype(BF16)
    k2 = k_ref[...].reshape(n, HEAD_DIM).astype(BF16)
    v2 = v_ref[...].reshape(n, HEAD_DIM).astype(BF16)
    _, lane, own_head = _page_lanes(page, heads)
    z = _dot_nt(q, k2)
    sp = _softplus(z)
    spm = jnp.where(own_head, sp, 0.0)
    suffix = spm
    shift = heads
    while shift < n:
        moved = pltpu.roll(suffix, n - shift, 1)
        suffix = suffix + jnp.where(lane + shift < n, moved, 0.0)
        shift *= 2
    between = -(suffix - spm) - keep_ref[...]
    wgt = jnp.where(own_head, jnp.exp(z - sp + between), 0.0)
    acc_ref[...] += _dot(wgt.astype(BF16), v2)
    keep_ref[...] += jnp.sum(spm, axis=1, keepdims=True)

    @pl.when(p == pl.num_programs(1) - 1)
    def _():
        o_ref[0] = _rms(acc_ref[...]) * g_ref[...]


def _sb_decode(page_table, q, cache_k, cache_v, layer, g):
    b, heads, hd = q.shape
    n_pages = page_table.shape[1]
    page = cache_k.shape[2]
    cache_spec = pl.BlockSpec((None, None, page, heads, hd),
                              lambda i, p, pt: (layer, pt[i, n_pages - 1 - p], 0, 0, 0))
    grid_spec = pltpu.PrefetchScalarGridSpec(
        num_scalar_prefetch=1,
        grid=(b, n_pages),
        in_specs=[pl.BlockSpec((1, heads, hd), lambda i, p, pt: (i, 0, 0)),
                  cache_spec, cache_spec,
                  pl.BlockSpec((heads, hd), lambda i, p, pt: (0, 0))],
        out_specs=pl.BlockSpec((1, heads, hd), lambda i, p, pt: (i, 0, 0)),
        scratch_shapes=[pltpu.VMEM((heads, 1), F32), pltpu.VMEM((heads, hd), F32)],
    )
    return pl.pallas_call(
        functools.partial(_sb_decode_kernel, page=page, heads=heads),
        grid_spec=grid_spec,
        out_shape=jax.ShapeDtypeStruct((b, heads, hd), F32),
        compiler_params=_params("arbitrary", "arbitrary"),
        name="sb_decode_attn",
    )(page_table, q, cache_k, cache_v, g)


def _moba_gate_kernel(pt_ref, q_ref, *refs, n_blocks, pages_per_block):
    del pt_ref
    k_refs = refs[:pages_per_block]
    sel_ref, gate_ref = refs[pages_per_block], refs[pages_per_block + 1]
    j = pl.program_id(1)
    ksum = k_refs[0][...].sum(axis=0)
    for r in k_refs[1:]:
        ksum = ksum + r[...].sum(axis=0)
    kmean = ksum * (1.0 / MOBA_BLOCK)
    gate = jnp.sum(q_ref[0] * kmean, axis=1, keepdims=True)
    gate_ref[j] = jnp.broadcast_to(gate, gate_ref.shape[1:])

    @pl.when(j == n_blocks - 1)
    def _():
        gates = gate_ref[...]
        blk = lax.broadcasted_iota(jnp.int32, gates.shape, 0)
        for r in range(MOBA_TOPK):
            best = jnp.max(gates, axis=0, keepdims=True)
            idx = jnp.min(jnp.where(gates == best, blk, n_blocks), axis=0, keepdims=True)
            sel_ref[0, r] = idx[0]
            gates = jnp.where(blk == idx, -jnp.inf, gates)


def _moba_gate(page_table, q_rot, cache_k, layer):
    b, heads, hd = q_rot.shape
    page = cache_k.shape[2]
    ppb = MOBA_BLOCK // page
    n_blocks = page_table.shape[1] // ppb
    cache_specs = [pl.BlockSpec((None, None, page, heads, hd),
                                functools.partial(lambda i, j, pt, o: (layer, pt[i, ppb * j + o], 0, 0, 0), o=o))
                   for o in range(ppb)]
    grid_spec = pltpu.PrefetchScalarGridSpec(
        num_scalar_prefetch=1,
        grid=(b, n_blocks),
        in_specs=[pl.BlockSpec((1, heads, hd), lambda i, j, pt: (i, 0, 0))] + cache_specs,
        out_specs=pl.BlockSpec((1, MOBA_TOPK, heads, LANES), lambda i, j, pt: (i, 0, 0, 0)),
        scratch_shapes=[pltpu.VMEM((n_blocks, heads, LANES), F32)],
    )
    sel = pl.pallas_call(
        functools.partial(_moba_gate_kernel, n_blocks=n_blocks, pages_per_block=ppb),
        grid_spec=grid_spec,
        out_shape=jax.ShapeDtypeStruct((b, MOBA_TOPK, heads, LANES), jnp.int32),
        compiler_params=_params("arbitrary", "arbitrary"),
        name="moba_decode_gate",
    )(page_table, q_rot, *([cache_k] * ppb))
    return sel[..., 0]


def _moba_decode_kernel(pt_ref, sel_ref, q_ref, kn_ref, vn_ref, k_ref, v_ref, g_ref, o_ref,
                        m_ref, l_ref, acc_ref, *, page, heads):
    del pt_ref, sel_ref
    h = pl.program_id(1)
    step = pl.program_id(2)
    q = q_ref[0] * HEAD_DIM ** -0.5

    @pl.when(jnp.logical_and(h == 0, step == 0))
    def _():
        m_ref[...] = jnp.sum(q * kn_ref[0], axis=1, keepdims=True)
        l_ref[...] = jnp.ones_like(l_ref)
        acc_ref[...] = vn_ref[0]

    n = page * heads
    k2 = k_ref[...].reshape(n, HEAD_DIM).astype(BF16)
    v2 = v_ref[...].reshape(n, HEAD_DIM).astype(BF16)
    row, _, own_head = _page_lanes(page, heads)
    live = jnp.logical_and(own_head, row == h)
    s = jnp.where(live, _dot_nt(q.astype(BF16), k2), -jnp.inf)
    m = m_ref[...]
    m_new = jnp.maximum(m, jnp.max(s, axis=1, keepdims=True))
    alpha = jnp.exp(m - m_new)
    p = jnp.exp(s - m_new)
    l_ref[...] = alpha * l_ref[...] + jnp.sum(p, axis=1, keepdims=True)
    acc_ref[...] = alpha * acc_ref[...] + _dot(p.astype(BF16), v2)
    m_ref[...] = m_new

    @pl.when(jnp.logical_and(h == heads - 1, step == pl.num_programs(2) - 1))
    def _():
        o_ref[0] = _rms(acc_ref[...] / l_ref[...]) * g_ref[...]


def _moba_decode(page_table, sel_flat, q_rot, k_new, v_new, cache_k, cache_v, layer, g):
    b, heads, hd = q_rot.shape
    page = cache_k.shape[2]
    ppb = MOBA_BLOCK // page
    steps = MOBA_TOPK * ppb

    def cache_map(i, h, s, pt, sel):
        blk = sel[(i * MOBA_TOPK + s // ppb) * heads + h]
        return (layer, pt[i, blk * ppb + s % ppb], 0, 0, 0)

    cache_spec = pl.BlockSpec((None, None, page, heads, hd), cache_map)
    row_spec = pl.BlockSpec((1, heads, hd), lambda i, h, s, pt, sel: (i, 0, 0))
    grid_spec = pltpu.PrefetchScalarGridSpec(
        num_scalar_prefetch=2,
        grid=(b, heads, steps),
        in_specs=[row_spec, row_spec, row_spec, cache_spec, cache_spec,
                  pl.BlockSpec((heads, hd), lambda i, h, s, pt, sel: (0, 0))],
        out_specs=row_spec,
        scratch_shapes=[pltpu.VMEM((heads, 1), F32), pltpu.VMEM((heads, 1), F32),
                        pltpu.VMEM((heads, hd), F32)],
    )
    return pl.pallas_call(
        functools.partial(_moba_decode_kernel, page=page, heads=heads),
        grid_spec=grid_spec,
        out_shape=jax.ShapeDtypeStruct((b, heads, hd), F32),
        compiler_params=_params("arbitrary", "arbitrary", "arbitrary"),
        name="moba_decode_attn",
    )(page_table, sel_flat, q_rot, k_new, v_new, cache_k, cache_v, g)


def _rope_tables(pos):
    half = HEAD_DIM // 2
    inv_freq = ROPE_THETA ** (-jnp.arange(half, dtype=F32) / half)
    ang = pos.astype(F32)[:, None] * inv_freq[None, :]
    cos, sin = jnp.cos(ang), jnp.sin(ang)
    zero = jnp.zeros_like(sin)
    reps = LANES // HEAD_DIM
    cos_t = jnp.tile(jnp.concatenate([cos, cos], axis=1), (1, reps))
    sin_a = jnp.tile(jnp.concatenate([-sin, zero], axis=1), (1, reps))
    sin_b = jnp.tile(jnp.concatenate([zero, sin], axis=1), (1, reps))
    return cos_t, sin_a, sin_b


def _row_tile(t, target):
    tm = min(t, target)
    while t % tm:
        tm //= 2
    return tm


def kernel(x_prompt, x_sample, c_prompt, c_sample, cache_sb_k, cache_sb_v, cache_moba_k, cache_moba_v, state_ffn_conv, page_table, w_ada, b_ada, g_attn, w_in, g_sb_out, g_moba_out, w_out, g_ffn, w_up, conv_w, conv_b, w_down, g_final):
    batch, seq, d = x_prompt.shape
    dec_b, dec_seq, _ = x_sample.shape
    depth = w_ada.shape[0]
    n_pages, page = page_table.shape[1], cache_sb_k.shape[2]
    heads = cache_sb_k.shape[3]
    w = heads * HEAD_DIM
    d_ff = w_down.shape[1]
    past_len = n_pages * page
    assert dec_seq == 1 and d == 2 * w and w_in.shape[2] == 6 * w
    assert seq % MOBA_BLOCK == 0 and past_len % MOBA_BLOCK == 0 and MOBA_BLOCK % page == 0
    assert past_len // MOBA_BLOCK >= MOBA_TOPK and d_ff % FFN_CHUNK == 0

    n_c = batch + dec_b
    c_all = jnp.concatenate([c_prompt, c_sample], axis=0)
    c_all = jnp.pad(c_all, ((0, -n_c % SUBLANES), (0, 0)))
    mod = _ada(c_all, w_ada, b_ada)

    rope_p = _rope_tables(jnp.arange(seq))
    rope_s = _rope_tables(jnp.full((dec_b,), past_len))
    nb = seq // MOBA_BLOCK
    nbp = -(-nb // LANES) * LANES
    tm_p = _row_tile(seq, 256)
    xs = x_sample.reshape(dec_b, d)
    xps = [x_prompt[b] for b in range(batch)]
    zeros_prev = jnp.zeros((SUBLANES, 2 * d_ff), F32)

    rows_p = [[] for _ in range(4)]
    rows_s = [[] for _ in range(4)]
    conv_p, conv_s = [], []
    for l in range(depth):
        wl = w_in[l]
        wqk = jnp.concatenate([wl[:, 3 * w:4 * w], wl[:, 4 * w:5 * w]], axis=1)
        wqk_hi = wqk.astype(BF16)
        wqk_lo = (wqk - wqk_hi.astype(F32)).astype(BF16)
        wts = (wl[:, :3 * w].astype(BF16), wqk_hi, wqk_lo, wl[:, 5 * w:].astype(BF16))
        wo_a, wo_b = w_out[l, :w].astype(BF16), w_out[l, w:].astype(BF16)
        wup, wdn = w_up[l].astype(BF16), w_down[l].astype(BF16)
        cw, cb = conv_w[l], conv_b[l].reshape(1, -1)
        g_a, g_f = g_attn[l].reshape(1, d), g_ffn[l].reshape(1, d)
        g_sb, g_mb = g_sb_out[l].reshape(1, w), g_moba_out[l].reshape(1, w)

        def mods(lo, hi):
            m = mod[l, lo:hi]
            return [m[:, k * d:(k + 1) * d] for k in range(6)]

        layer_rows = [[] for _ in range(4)]
        layer_conv = []
        for b in range(batch):
            sh1, sc1, ga1, sh2, sc2, ga2 = mods(b, b + 1)
            (k_sb, v_sb, k_mb, v_mb, q_mbf, q_sb16, k_sb16, v_sb16, q_mb16, k_mb16,
             v_mb16) = _proj(xps[b], sc1, sh1, g_a, wts, rope_p, tm_p)
            o_sb = _sb_prompt(q_sb16, k_sb16, v_sb16, g_sb, MOBA_BLOCK)
            kmean = jnp.pad(_kmean(k_mb), ((0, nbp - nb), (0, 0)))
            o_mb = _moba_prompt(q_mbf, q_mb16, k_mb16, v_mb16, kmean, g_mb)
            x_mid, h2 = _outproj(o_sb, o_mb, wo_a, wo_b, xps[b], ga1, g_f, sc2, sh2, tm_p)
            xps[b], state8 = _ffn_prompt(h2, wup, cw, cb, wdn, x_mid, ga2, zeros_prev, tm_p)
            for lst, a in zip(layer_rows, (k_sb, v_sb, k_mb, v_mb)):
                lst.append(a.reshape(seq, heads, HEAD_DIM))
            layer_conv.append(state8[SUBLANES - (CONV_W - 1):])
        for lst, parts in zip(rows_p, layer_rows):
            lst.append(jnp.stack(parts))
        conv_p.append(jnp.stack(layer_conv))

        sh1, sc1, ga1, sh2, sc2, ga2 = mods(batch, n_c)
        (k_sb, v_sb, k_mb, v_mb, q_mbf, q_sb16, _, _, _, _, _) = _proj(
            xs, sc1, sh1, g_a, wts, rope_s, dec_b)
        as_heads = lambda a: a.reshape(dec_b, heads, HEAD_DIM)
        q_sb = q_sb16.astype(F32) * HEAD_DIM ** 0.5
        o_sb = _sb_decode(page_table, as_heads(q_sb), cache_sb_k, cache_sb_v, l,
                          g_sb.reshape(heads, HEAD_DIM))
        sel = _moba_gate(page_table, as_heads(q_mbf), cache_moba_k, l)
        o_mb = _moba_decode(page_table, sel.reshape(-1), as_heads(q_mbf), as_heads(k_mb),
                            as_heads(v_mb), cache_moba_k, cache_moba_v, l,
                            g_mb.reshape(heads, HEAD_DIM))
        x_mid, h2 = _outproj(o_sb.reshape(dec_b, w).astype(BF16), o_mb.reshape(dec_b, w).astype(BF16),
                             wo_a, wo_b, xs, ga1, g_f, sc2, sh2, dec_b)
        prev = state_ffn_conv[l]
        xs, u_new = _ffn_decode(h2, wup, cw, cb, wdn, x_mid, ga2, prev[:, 0], prev[:, 1])
        for lst, a in zip(rows_s, (k_sb, v_sb, k_mb, v_mb)):
            lst.append(a.reshape(dec_b, 1, heads, HEAD_DIM))
        conv_s.append(jnp.stack([prev[:, 1], u_new], axis=1))

    g_fin = g_final.reshape(1, d)
    y_prompt = jnp.stack([_final_norm(xp, g_fin, tm_p) for xp in xps])
    y_sample = _final_norm(xs, g_fin, dec_b).reshape(dec_b, 1, d)
    return (y_prompt, y_sample,
            jnp.stack(rows_p[0]), jnp.stack(rows_p[1]), jnp.stack(rows_p[2]), jnp.stack(rows_p[3]),
            jnp.stack(conv_p),
            jnp.stack(rows_s[0]), jnp.stack(rows_s[1]), jnp.stack(rows_s[2]), jnp.stack(rows_s[3]),
            jnp.stack(conv_s))
```

```python
import functools

import jax
import jax.numpy as jnp
from jax import lax
from jax.experimental import pallas as pl
from jax.experimental.pallas import tpu as pltpu

HEAD_DIM = 64
MOBA_BLOCK = 256
MOBA_TOPK = 3
ROPE_THETA = 10000.0
CONV_W = 3
RMS_EPS = 1e-6

LANES = 128
SUBLANES = 8
VMEM_LIMIT_BYTES = 56 * 1024 * 1024
NEG_BIG = -1e30
SB_EXIT = 110.0
MOBA_KEY_CHUNK = 4
DECODE_PAGE_GROUP = 4

F32 = jnp.float32
BF16 = jnp.bfloat16
NT_DIMS = (((1,), (1,)), ((), ()))


def _params(*sem):
    return pltpu.CompilerParams(dimension_semantics=sem, vmem_limit_bytes=VMEM_LIMIT_BYTES)


def _dot(a, b):
    return jnp.dot(a, b, preferred_element_type=F32)


def _dot_nt(a, b):
    return lax.dot_general(a, b, NT_DIMS, preferred_element_type=F32)


def _split(x):
    hi = x.astype(BF16)
    lo = (x - hi.astype(F32)).astype(BF16)
    return hi, lo


def _rms(x):
    return x * lax.rsqrt(jnp.mean(x * x, axis=-1, keepdims=True) + RMS_EPS)


def _softplus(z):
    return jnp.maximum(z, 0.0) + jnp.log(1.0 + jnp.exp(-jnp.abs(z)))


def _row_spec(rows, tm, width):
    if rows == 1:
        return pl.BlockSpec((1, width), lambda i: (0, 0))
    return pl.BlockSpec((tm, width), lambda i: (i, 0))


def _const_spec(shape):
    return pl.BlockSpec(shape, lambda *_: (0,) * len(shape), pipeline_mode=pl.Buffered(1))


def _ada_kernel(c_ref, w_ref, b_ref, o_ref):
    c = c_ref[...]
    s = c * (1.0 / (1.0 + jnp.exp(-c)))
    o_ref[0] = jnp.dot(s, w_ref[0], preferred_element_type=F32,
                       precision=lax.Precision.HIGHEST) + b_ref[0]


def _ada(c_all, w_ada, b_ada):
    depth, d, n = w_ada.shape
    r = c_all.shape[0]
    tn = 1536 if n % 1536 == 0 else n
    return pl.pallas_call(
        _ada_kernel,
        grid=(depth, n // tn),
        in_specs=[pl.BlockSpec((r, d), lambda l, j: (0, 0)),
                  pl.BlockSpec((1, d, tn), lambda l, j: (l, 0, j)),
                  pl.BlockSpec((1, 1, tn), lambda l, j: (l, 0, j))],
        out_specs=pl.BlockSpec((1, r, tn), lambda l, j: (l, 0, j)),
        out_shape=jax.ShapeDtypeStruct((depth, r, n), F32),
        compiler_params=_params("arbitrary", "arbitrary"),
        name="ada_mod",
    )(c_all, w_ada, b_ada.reshape(depth, 1, n))


def _rope(x, cos, sin_a, sin_b):
    outs = []
    for c in range(x.shape[1] // LANES):
        xc = x[:, c * LANES:(c + 1) * LANES]
        outs.append(xc * cos + pltpu.roll(xc, LANES - HEAD_DIM // 2, 1) * sin_a
                    + pltpu.roll(xc, HEAD_DIM // 2, 1) * sin_b)
    return jnp.concatenate(outs, axis=1)


def _proj_kernel(x_ref, sc_ref, sh_ref, g_ref, wsb_ref, wqk_hi_ref, wqk_lo_ref, wmv_ref,
                 cos_ref, sa_ref, sb_ref,
                 ksb_ref, vsb_ref, kmb_ref, vmb_ref, qmbf_ref,
                 qsb16_ref, ksb16_ref, vsb16_ref, qmb16_ref, kmb16_ref, vmb16_ref, *, w):
    h = _rms(x_ref[...]) * g_ref[...] * (1.0 + sc_ref[...]) + sh_ref[...]
    h_hi, h_lo = _split(h)
    scale = HEAD_DIM ** -0.5

    sb = _dot(h_hi, wsb_ref[...])
    q_sb, k_sb, v_sb = sb[:, :w], sb[:, w:2 * w], sb[:, 2 * w:]
    ksb_ref[...] = k_sb
    vsb_ref[...] = v_sb
    qsb16_ref[...] = (q_sb * scale).astype(BF16)
    ksb16_ref[...] = k_sb.astype(BF16)
    vsb16_ref[...] = v_sb.astype(BF16)

    w_hi = wqk_hi_ref[...]
    qk = _dot(h_hi, w_hi) + _dot(h_hi, wqk_lo_ref[...]) + _dot(h_lo, w_hi)
    cos, sa, sb_t = cos_ref[...], sa_ref[...], sb_ref[...]
    q_mb = _rope(qk[:, :w], cos, sa, sb_t)
    k_mb = _rope(qk[:, w:], cos, sa, sb_t)
    v_mb = _dot(h_hi, wmv_ref[...])
    kmb_ref[...] = k_mb
    vmb_ref[...] = v_mb
    qmbf_ref[...] = q_mb
    qmb16_ref[...] = (q_mb * scale).astype(BF16)
    kmb16_ref[...] = k_mb.astype(BF16)
    vmb16_ref[...] = v_mb.astype(BF16)


def _proj(x, sc, sh, g, wts, rope_tabs, tm):
    t, d = x.shape
    wsb, wqk_hi, wqk_lo, wmv = wts
    w = wmv.shape[1]
    cos, sa, sb = rope_tabs
    row = lambda width: pl.BlockSpec((tm, width), lambda i: (i, 0))
    f32_out = jax.ShapeDtypeStruct((t, w), F32)
    b16_out = jax.ShapeDtypeStruct((t, w), BF16)
    return pl.pallas_call(
        functools.partial(_proj_kernel, w=w),
        grid=(t // tm,),
        in_specs=[row(d), _row_spec(sc.shape[0], tm, d), _row_spec(sh.shape[0], tm, d),
                  _const_spec((1, d)), _const_spec(wsb.shape), _const_spec(wqk_hi.shape),
                  _const_spec(wqk_lo.shape), _const_spec(wmv.shape),
                  row(LANES), row(LANES), row(LANES)],
        out_specs=[row(w)] * 11,
        out_shape=[f32_out] * 5 + [b16_out] * 6,
        compiler_params=_params("arbitrary"),
        name="qkv_proj",
    )(x, sc, sh, g, wsb, wqk_hi, wqk_lo, wmv, cos, sa, sb)


def _head_norm_pair(o, first, g):
    sq = o * o
    ms_a = jnp.sum(jnp.where(first, sq, 0.0), axis=1, keepdims=True) * (1.0 / HEAD_DIM)
    ms_b = jnp.sum(jnp.where(first, 0.0, sq), axis=1, keepdims=True) * (1.0 / HEAD_DIM)
    r = jnp.where(first, lax.rsqrt(ms_a + RMS_EPS), lax.rsqrt(ms_b + RMS_EPS))
    return o * r * g


def _sb_prompt_kernel(q_ref, k_ref, v_ref, g_ref, o_ref, *, tq):
    i = pl.program_id(1)
    q = q_ref[...]
    lane = lax.broadcasted_iota(jnp.int32, (tq, LANES), 1)
    first = lane < HEAD_DIM
    zero = jnp.zeros_like(q)
    qs = (jnp.where(first, q, zero), jnp.where(first, zero, q))
    r_i = lax.broadcasted_iota(jnp.int32, (tq, tq), 0)
    c_i = lax.broadcasted_iota(jnp.int32, (tq, tq), 1)
    neg_upper = jnp.where(r_i > c_i, -1.0, 0.0).astype(BF16)
    causal = c_i < r_i

    def tile(j, state, diag):
        start = pl.multiple_of(j * tq, tq)
        kt = k_ref[pl.ds(start, tq), :]
        vt = v_ref[pl.ds(start, tq), :]
        out = []
        for a in range(2):
            keep_sum, acc = state[2 * a], state[2 * a + 1]
            z = _dot_nt(qs[a], kt)
            sp = _softplus(z)
            spm = jnp.where(causal, sp, 0.0) if diag else sp
            hi, lo = _split(spm)
            between = _dot(hi, neg_upper) + _dot(lo, neg_upper)
            wgt = jnp.exp(z - sp + between - keep_sum)
            if diag:
                wgt = jnp.where(causal, wgt, 0.0)
            acc = acc + _dot(wgt.astype(BF16), vt)
            keep_sum = keep_sum + jnp.sum(spm, axis=1, keepdims=True)
            out += [keep_sum, acc]
        return tuple(out)

    def alive(st):
        return jnp.min(jnp.minimum(st[0], st[2])) < SB_EXIT

    def step(carry):
        st = tile(i - 1 - carry[0], carry[2:], False)
        return (carry[0] + 1, alive(st)) + st

    zc = jnp.zeros((tq, 1), F32)
    za = jnp.zeros((tq, LANES), F32)
    state = tile(i, (zc, za, zc, za), True)
    carry = lax.while_loop(lambda c: jnp.logical_and(c[0] < i, c[1]), step,
                           (jnp.int32(0), alive(state)) + state)
    state = carry[2:]
    o = jnp.where(first, state[1], state[3])
    o_ref[...] = _head_norm_pair(o, first, g_ref[...]).astype(o_ref.dtype)


def _sb_prompt(q16, k16, v16, g, tq):
    t, w = q16.shape
    return pl.pallas_call(
        functools.partial(_sb_prompt_kernel, tq=tq),
        grid=(w // LANES, t // tq),
        in_specs=[pl.BlockSpec((tq, LANES), lambda p, i: (i, p)),
                  pl.BlockSpec((t, LANES), lambda p, i: (0, p)),
                  pl.BlockSpec((t, LANES), lambda p, i: (0, p)),
                  pl.BlockSpec((1, LANES), lambda p, i: (0, p))],
        out_specs=pl.BlockSpec((tq, LANES), lambda p, i: (i, p)),
        out_shape=jax.ShapeDtypeStruct((t, w), BF16),
        compiler_params=_params("arbitrary", "arbitrary"),
        name="sb_prompt_attn",
    )(q16, k16, v16, g)


def _kmean_kernel(k_ref, o_ref):
    o_ref[0] = jnp.mean(k_ref[...], axis=0, keepdims=True)


def _kmean(k_mb):
    t, w = k_mb.shape
    nb = t // MOBA_BLOCK
    return pl.pallas_call(
        _kmean_kernel,
        grid=(nb,),
        in_specs=[pl.BlockSpec((MOBA_BLOCK, w), lambda n: (n, 0))],
        out_specs=pl.BlockSpec((1, 1, w), lambda n: (n, 0, 0)),
        out_shape=jax.ShapeDtypeStruct((nb, 1, w), F32),
        compiler_params=_params("arbitrary"),
        name="moba_kmean",
    )(k_mb).reshape(nb, w)


def _moba_prompt_kernel(qf_ref, q_ref, k_ref, v_ref, km_ref, g_ref, o_ref, *, tq, nbp, kc):
    i = pl.program_id(1)
    q = q_ref[...]
    qf = qf_ref[...]
    km_hi, km_lo = _split(km_ref[...])
    lane = lax.broadcasted_iota(jnp.int32, (tq, LANES), 1)
    first = lane < HEAD_DIM
    col = lax.broadcasted_iota(jnp.int32, (tq, nbp), 1)
    past = col < i
    r_i = lax.broadcasted_iota(jnp.int32, (tq, tq), 0)
    c_i = lax.broadcasted_iota(jnp.int32, (tq, tq), 1)
    own_ok = c_i <= r_i
    blk_row = lax.broadcasted_iota(jnp.int32, (nbp, kc * tq), 0)
    col_blk = lax.broadcasted_iota(jnp.int32, (nbp, kc * tq), 1) // tq

    own_start = pl.multiple_of(i * tq, tq)
    k_own = k_ref[pl.ds(own_start, tq), :]
    v_own = v_ref[pl.ds(own_start, tq), :]

    qs, biases, state = [], [], []
    for a in range(2):
        mask = first if a == 0 else jnp.logical_not(first)
        qa = jnp.where(mask, q, jnp.zeros_like(q))
        qs.append(qa)
        qf_hi, qf_lo = _split(jnp.where(mask, qf, 0.0))
        gate = _dot_nt(qf_hi, km_hi) + _dot_nt(qf_hi, km_lo) + _dot_nt(qf_lo, km_hi)
        gate = jnp.where(past, gate, -jnp.inf)
        bias = jnp.full((tq, nbp), NEG_BIG, F32)
        for _ in range(MOBA_TOPK):
            best = jnp.max(gate, axis=1, keepdims=True)
            idx = jnp.min(jnp.where(gate == best, col, nbp), axis=1, keepdims=True)
            hit = col == idx
            bias = jnp.where(jnp.logical_and(hit, past), 0.0, bias)
            gate = jnp.where(hit, -jnp.inf, gate)
        biases.append(bias.astype(BF16))
        s = jnp.where(own_ok, _dot_nt(qa, k_own), -jnp.inf)
        m = jnp.max(s, axis=1, keepdims=True)
        p = jnp.exp(s - m)
        state += [m, jnp.sum(p, axis=1, keepdims=True), _dot(p.astype(BF16), v_own)]

    def body(c, st):
        start = pl.multiple_of(c * (kc * tq), kc * tq)
        kt = k_ref[pl.ds(start, kc * tq), :]
        vt = v_ref[pl.ds(start, kc * tq), :]
        onehot = jnp.where(blk_row == c * kc + col_blk, 1.0, 0.0).astype(BF16)
        out = []
        for a in range(2):
            m, l, acc = st[3 * a], st[3 * a + 1], st[3 * a + 2]
            s = _dot_nt(qs[a], kt) + _dot(biases[a], onehot)
            m_new = jnp.maximum(m, jnp.max(s, axis=1, keepdims=True))
            alpha = jnp.exp(m - m_new)
            p = jnp.exp(s - m_new)
            l = alpha * l + jnp.sum(p, axis=1, keepdims=True)
            acc = alpha * acc + _dot(p.astype(BF16), vt)
            out += [m_new, l, acc]
        return tuple(out)

    st = lax.fori_loop(0, (i + kc - 1) // kc, body, tuple(state))
    o = jnp.where(first, st[2] / st[1], st[5] / st[4])
    o_ref[...] = _head_norm_pair(o, first, g_ref[...]).astype(o_ref.dtype)


def _moba_prompt(qf, q16, k16, v16, kmean_p, g):
    t, w = q16.shape
    tq = MOBA_BLOCK
    nbp = kmean_p.shape[0]
    kc = MOBA_KEY_CHUNK
    while (t // tq) % kc:
        kc //= 2
    return pl.pallas_call(
        functools.partial(_moba_prompt_kernel, tq=tq, nbp=nbp, kc=kc),
        grid=(w // LANES, t // tq),
        in_specs=[pl.BlockSpec((tq, LANES), lambda p, i: (i, p)),
                  pl.BlockSpec((tq, LANES), lambda p, i: (i, p)),
                  pl.BlockSpec((t, LANES), lambda p, i: (0, p)),
                  pl.BlockSpec((t, LANES), lambda p, i: (0, p)),
                  pl.BlockSpec((nbp, LANES), lambda p, i: (0, p)),
                  pl.BlockSpec((1, LANES), lambda p, i: (0, p))],
        out_specs=pl.BlockSpec((tq, LANES), lambda p, i: (i, p)),
        out_shape=jax.ShapeDtypeStruct((t, w), BF16),
        compiler_params=_params("arbitrary", "arbitrary"),
        name="moba_prompt_attn",
    )(qf, q16, k16, v16, kmean_p, g)


def _outproj_kernel(osb_ref, omb_ref, wa_ref, wb_ref, x_ref, ga_ref, g_ref, sc_ref, sh_ref,
                    xo_ref, h_ref):
    mix = _dot(osb_ref[...], wa_ref[...]) + _dot(omb_ref[...], wb_ref[...])
    x = x_ref[...] + ga_ref[...] * mix
    xo_ref[...] = x
    h_ref[...] = (_rms(x) * g_ref[...] * (1.0 + sc_ref[...]) + sh_ref[...]).astype(h_ref.dtype)


def _outproj(o_sb, o_mb, w_a, w_b, x, ga, g, sc, sh, tm):
    t, d = x.shape
    w = o_sb.shape[1]
    row = lambda width: pl.BlockSpec((tm, width), lambda i: (i, 0))
    mod = lambda a: _row_spec(a.shape[0], tm, d)
    return pl.pallas_call(
        _outproj_kernel,
        grid=(t // tm,),
        in_specs=[row(w), row(w), _const_spec(w_a.shape), _const_spec(w_b.shape), row(d),
                  mod(ga), _const_spec((1, d)), mod(sc), mod(sh)],
        out_specs=[row(d), row(d)],
        out_shape=[jax.ShapeDtypeStruct((t, d), F32), jax.ShapeDtypeStruct((t, d), BF16)],
        compiler_params=_params("arbitrary"),
        name="out_proj",
    )(o_sb, o_mb, w_a, w_b, x, ga, g, sc, sh)


FFN_CHUNK = 256


def _silu_gate(val, gate):
    return val * gate * (1.0 / (1.0 + jnp.exp(-gate)))


def _ffn_prompt_kernel(h_ref, wup_ref, cw_ref, cb_ref, wdn_ref, x_ref, ga_ref, prev_ref,
                       xo_ref, state_ref, carry_ref, *, d_ff, tm):
    i = pl.program_id(0)

    @pl.when(i == 0)
    def _():
        carry_ref[...] = prev_ref[...]

    h = h_ref[...]
    row = lax.broadcasted_iota(jnp.int32, (tm, FFN_CHUNK), 0)

    def conv(off):
        cols = pl.ds(off, FFN_CHUNK)
        u = _dot(h, wup_ref[:, cols])
        prev = carry_ref[:, cols]
        p1, p2 = prev[SUBLANES - 1:SUBLANES, :], prev[SUBLANES - 2:SUBLANES - 1, :]
        u1 = jnp.where(row == 0, p1, pltpu.roll(u, 1, 0))
        u2 = jnp.where(row == 0, p2, jnp.where(row == 1, p1, pltpu.roll(u, 2, 0)))
        carry_ref[:, cols] = u[tm - SUBLANES:, :]
        cw = cw_ref[:, cols]
        return cb_ref[:, cols] + u2 * cw[0:1, :] + u1 * cw[1:2, :] + u * cw[2:3, :]

    acc = jnp.zeros((tm, xo_ref.shape[1]), F32)
    for c in range(d_ff // FFN_CHUNK):
        off = c * FFN_CHUNK
        act = _silu_gate(conv(off), conv(d_ff + off))
        acc = acc + _dot(act.astype(BF16), wdn_ref[pl.ds(off, FFN_CHUNK), :])
    xo_ref[...] = x_ref[...] + ga_ref[...] * acc

    @pl.when(i == pl.num_programs(0) - 1)
    def _():
        state_ref[...] = carry_ref[...]


def _ffn_prompt(h, w_up, conv_w, conv_b, w_down, x, ga, prev8, tm):
    t, d = x.shape
    d_ff = w_down.shape[0]
    row = lambda width: pl.BlockSpec((tm, width), lambda i: (i, 0))
    return pl.pallas_call(
        functools.partial(_ffn_prompt_kernel, d_ff=d_ff, tm=tm),
        grid=(t // tm,),
        in_specs=[row(d), _const_spec(w_up.shape), _const_spec(conv_w.shape),
                  _const_spec(conv_b.shape), _const_spec(w_down.shape), row(d),
                  _row_spec(ga.shape[0], tm, d), _const_spec(prev8.shape)],
        out_specs=[row(d), pl.BlockSpec(prev8.shape, lambda i: (0, 0))],
        out_shape=[jax.ShapeDtypeStruct((t, d), F32), jax.ShapeDtypeStruct(prev8.shape, F32)],
        scratch_shapes=[pltpu.VMEM(prev8.shape, F32)],
        compiler_params=_params("arbitrary"),
        name="conv_ffn_prompt",
    )(h, w_up, conv_w, conv_b, w_down, x, ga, prev8)


def _ffn_decode_kernel(h_ref, wup_ref, cw_ref, cb_ref, wdn_ref, x_ref, ga_ref, p0_ref, p1_ref,
                       xo_ref, u_ref, *, d_ff):
    h = h_ref[...]

    def conv(off):
        cols = pl.ds(off, FFN_CHUNK)
        u = _dot(h, wup_ref[:, cols])
        u_ref[:, cols] = u
        cw = cw_ref[:, cols]
        return (cb_ref[:, cols] + p0_ref[:, cols] * cw[0:1, :] + p1_ref[:, cols] * cw[1:2, :]
                + u * cw[2:3, :])

    acc = jnp.zeros(xo_ref.shape, F32)
    for c in range(d_ff // FFN_CHUNK):
        off = c * FFN_CHUNK
        act = _silu_gate(conv(off), conv(d_ff + off))
        acc = acc + _dot(act.astype(BF16), wdn_ref[pl.ds(off, FFN_CHUNK), :])
    xo_ref[...] = x_ref[...] + ga_ref[...] * acc


def _ffn_decode(h, w_up, conv_w, conv_b, w_down, x, ga, prev0, prev1):
    b, d = x.shape
    d_ff = w_down.shape[0]
    full = lambda a: pl.BlockSpec(a.shape, lambda i: (0,) * a.ndim)
    args = (h, w_up, conv_w, conv_b, w_down, x, ga, prev0, prev1)
    return pl.pallas_call(
        functools.partial(_ffn_decode_kernel, d_ff=d_ff),
        grid=(1,),
        in_specs=[full(a) for a in args],
        out_specs=[pl.BlockSpec((b, d), lambda i: (0, 0)),
                   pl.BlockSpec((b, 2 * d_ff), lambda i: (0, 0))],
        out_shape=[jax.ShapeDtypeStruct((b, d), F32), jax.ShapeDtypeStruct((b, 2 * d_ff), F32)],
        compiler_params=_params("arbitrary"),
        name="conv_ffn_decode",
    )(*args)


def _final_norm_kernel(x_ref, g_ref, o_ref):
    o_ref[...] = _rms(x_ref[...]) * g_ref[...]


def _final_norm(x, g, tm):
    t, d = x.shape
    return pl.pallas_call(
        _final_norm_kernel,
        grid=(t // tm,),
        in_specs=[pl.BlockSpec((tm, d), lambda i: (i, 0)), pl.BlockSpec((1, d), lambda i: (0, 0))],
        out_specs=pl.BlockSpec((tm, d), lambda i: (i, 0)),
        out_shape=jax.ShapeDtypeStruct((t, d), F32),
        compiler_params=_params("arbitrary"),
        name="final_norm",
    )(x, g)


def _page_scores(qb, k_page, heads):
    sub = lax.broadcasted_iota(jnp.int32, (heads, k_page.shape[-1]), 0)
    z = jnp.zeros(sub.shape, F32)
    for h in range(heads):
        row = jnp.sum(qb[h] * k_page[h], axis=0, keepdims=True)
        z = jnp.where(sub == h, row, z)
    return z


def _sb_decode_kernel(pt_ref, qb_ref, k_hbm, v_hbm, g_ref, o_ref, kbuf, vbuf, acc_ref, sem,
                      *, layer, n_pages, group, heads):
    b = pl.program_id(0)
    n_chunks = n_pages // group
    page = kbuf.shape[-1]
    lane = lax.broadcasted_iota(jnp.int32, (heads, page), 1)
    qb = qb_ref.at[0]

    def copies(c, slot):
        out = []
        for g in range(group):
            pg = pt_ref[b, c * group + g]
            out.append(pltpu.make_async_copy(k_hbm.at[layer, pg], kbuf.at[slot, g], sem.at[0, slot]))
            out.append(pltpu.make_async_copy(v_hbm.at[layer, pg], vbuf.at[slot, g], sem.at[1, slot]))
        return out

    def start(c, slot):
        for cp in copies(c, slot):
            cp.start()

    def wait(c, slot):
        for cp in copies(c, slot):
            cp.wait()

    def page_update(slot, g, keep):
        z = _page_scores(qb, kbuf.at[slot, g], heads) * HEAD_DIM ** -0.5
        sp = _softplus(z)
        suffix = sp
        shift = 1
        while shift < page:
            moved = pltpu.roll(suffix, page - shift, 1)
            suffix = suffix + jnp.where(lane + shift < page, moved, 0.0)
            shift *= 2
        wgt = jnp.exp(z - sp - (suffix - sp) - keep)
        for h in range(heads):
            acc_ref[h] += wgt[h:h + 1, :] * vbuf[slot, g, h]
        return keep + jnp.sum(sp, axis=1, keepdims=True)

    def step(carry):
        c, _, keep = carry
        slot = c % 2

        @pl.when(c > 0)
        def _():
            start(c - 1, 1 - slot)

        wait(c, slot)
        for g in reversed(range(group)):
            keep = page_update(slot, g, keep)
        return c - 1, jnp.min(keep) < SB_EXIT, keep

    acc_ref[...] = jnp.zeros_like(acc_ref)
    last = n_chunks - 1
    start(last, last % 2)
    c_end, _, _ = lax.while_loop(lambda cr: jnp.logical_and(cr[0] >= 0, cr[1]), step,
                                 (jnp.int32(last), jnp.bool_(True), jnp.zeros((heads, 1), F32)))

    @pl.when(c_end >= 0)
    def _():
        wait(c_end, c_end % 2)

    for h in range(heads):
        o = jnp.sum(acc_ref[h], axis=1, keepdims=True)
        o_ref[0, h] = o * lax.rsqrt(jnp.mean(o * o, axis=0, keepdims=True) + RMS_EPS) * g_ref[h]


def _sb_decode(page_table, qb, cache_k, cache_v, layer, g):
    b, heads, hd, page = qb.shape
    n_pages = page_table.shape[1]
    group = DECODE_PAGE_GROUP
    while n_pages % group:
        group //= 2
    grid_spec = pltpu.PrefetchScalarGridSpec(
        num_scalar_prefetch=1,
        grid=(b,),
        in_specs=[pl.BlockSpec((1, heads, hd, page), lambda i, pt: (i, 0, 0, 0)),
                  pl.BlockSpec(memory_space=pl.ANY), pl.BlockSpec(memory_space=pl.ANY),
                  pl.BlockSpec((heads, hd, 1), lambda i, pt: (0, 0, 0))],
        out_specs=pl.BlockSpec((1, heads, hd, 1), lambda i, pt: (i, 0, 0, 0)),
        scratch_shapes=[pltpu.VMEM((2, group, heads, hd, page), F32),
                        pltpu.VMEM((2, group, heads, hd, page), F32),
                        pltpu.VMEM((heads, hd, page), F32),
                        pltpu.SemaphoreType.DMA((2, 2))],
    )
    return pl.pallas_call(
        functools.partial(_sb_decode_kernel, layer=layer, n_pages=n_pages, group=group, heads=heads),
        grid_spec=grid_spec,
        out_shape=jax.ShapeDtypeStruct((b, heads, hd, 1), F32),
        compiler_params=_params("arbitrary"),
        name="sb_decode_attn",
    )(page_table, qb, cache_k, cache_v, g)


def _moba_gate_kernel(pt_ref, qb_ref, *refs, group, ppb, heads):
    del pt_ref
    k_refs = refs[:group]
    sel_ref, gate_ref = refs[group], refs[group + 1]
    s = pl.program_id(1)
    page = k_refs[0].shape[-1]
    lane = lax.broadcasted_iota(jnp.int32, (heads, page), 1)
    qb = qb_ref.at[0]

    @pl.when(s == 0)
    def _():
        gate_ref[...] = jnp.full(gate_ref.shape, -jnp.inf, F32)

    for blk in range(group // ppb):
        z = _page_scores(qb, k_refs[blk * ppb], heads)
        for o in range(1, ppb):
            z = z + _page_scores(qb, k_refs[blk * ppb + o], heads)
        gate = jnp.sum(z, axis=1, keepdims=True) * (1.0 / MOBA_BLOCK)
        gate_ref[...] = jnp.where(lane == s * (group // ppb) + blk, gate, gate_ref[...])

    @pl.when(s == pl.num_programs(1) - 1)
    def _():
        gates = gate_ref[...]
        for r in range(MOBA_TOPK):
            best = jnp.max(gates, axis=1, keepdims=True)
            idx = jnp.min(jnp.where(gates == best, lane, page), axis=1, keepdims=True)
            sel_ref[0, r] = jnp.broadcast_to(idx, (heads, page))
            gates = jnp.where(lane == idx, -jnp.inf, gates)


def _moba_gate(page_table, qb, cache_k, layer):
    b, heads, hd, page = qb.shape
    n_pages = page_table.shape[1]
    ppb = MOBA_BLOCK // page
    assert n_pages // ppb <= page
    group = max(ppb, DECODE_PAGE_GROUP)
    while n_pages % group:
        group -= ppb
    cache_specs = [pl.BlockSpec((None, None, heads, hd, page),
                                functools.partial(lambda i, s, pt, o: (layer, pt[i, group * s + o], 0, 0, 0), o=o))
                   for o in range(group)]
    grid_spec = pltpu.PrefetchScalarGridSpec(
        num_scalar_prefetch=1,
        grid=(b, n_pages // group),
        in_specs=[pl.BlockSpec((1, heads, hd, page), lambda i, s, pt: (i, 0, 0, 0))] + cache_specs,
        out_specs=pl.BlockSpec((1, MOBA_TOPK, heads, page), lambda i, s, pt: (i, 0, 0, 0)),
        scratch_shapes=[pltpu.VMEM((heads, page), F32)],
    )
    sel = pl.pallas_call(
        functools.partial(_moba_gate_kernel, group=group, ppb=ppb, heads=heads),
        grid_spec=grid_spec,
        out_shape=jax.ShapeDtypeStruct((b, MOBA_TOPK, heads, page), jnp.int32),
        compiler_params=_params("arbitrary", "arbitrary"),
        name="moba_decode_gate",
    )(page_table, qb, *([cache_k] * group))
    return sel[..., 0]


def _moba_decode_kernel(pt_ref, sel_ref, qb_ref, kn_ref, vn_ref, *refs, n_sel):
    del pt_ref, sel_ref
    k_refs, v_refs = refs[:n_sel], refs[n_sel:2 * n_sel]
    g_ref, o_ref = refs[2 * n_sel], refs[2 * n_sel + 1]
    scale = HEAD_DIM ** -0.5
    qb = qb_ref[...]
    scores = [jnp.sum(qb * r[...], axis=0, keepdims=True) * scale for r in k_refs]
    s_own = jnp.sum(qb[:, :1] * kn_ref[...], axis=0, keepdims=True) * scale
    m = s_own
    for s in scores:
        m = jnp.maximum(m, jnp.max(s, axis=1, keepdims=True))
    p_own = jnp.exp(s_own - m)
    l = p_own
    acc = jnp.zeros(qb.shape, F32)
    for s, v_ref in zip(scores, v_refs):
        p = jnp.exp(s - m)
        l = l + jnp.sum(p, axis=1, keepdims=True)
        acc = acc + p * v_ref[...]
    o = (jnp.sum(acc, axis=1, keepdims=True) + p_own * vn_ref[...]) / l
    o_ref[...] = o * lax.rsqrt(jnp.mean(o * o, axis=0, keepdims=True) + RMS_EPS) * g_ref[...]


def _moba_decode(page_table, sel_flat, qb, k_new, v_new, cache_k, cache_v, layer, g):
    b, heads, hd, page = qb.shape
    ppb = MOBA_BLOCK // page
    n_sel = MOBA_TOPK * ppb

    def cache_map(i, h, pt, sel, n):
        blk = sel[(i * MOBA_TOPK + n // ppb) * heads + h]
        return (layer, pt[i, blk * ppb + n % ppb], h, 0, 0)

    cache_specs = [pl.BlockSpec((None, None, None, hd, page), functools.partial(cache_map, n=n))
                   for n in range(n_sel)]
    head_spec = lambda last: pl.BlockSpec((None, None, hd, last), lambda i, h, pt, sel: (i, h, 0, 0))
    grid_spec = pltpu.PrefetchScalarGridSpec(
        num_scalar_prefetch=2,
        grid=(b, heads),
        in_specs=[head_spec(page), head_spec(1), head_spec(1)] + cache_specs + cache_specs
                 + [pl.BlockSpec((None, hd, 1), lambda i, h, pt, sel: (h, 0, 0))],
        out_specs=head_spec(1),
    )
    return pl.pallas_call(
        functools.partial(_moba_decode_kernel, n_sel=n_sel),
        grid_spec=grid_spec,
        out_shape=jax.ShapeDtypeStruct((b, heads, hd, 1), F32),
        compiler_params=_params("arbitrary", "arbitrary"),
        name="moba_decode_attn",
    )(page_table, sel_flat, qb, k_new, v_new, *([cache_k] * n_sel), *([cache_v] * n_sel), g)


def _rope_tables(pos):
    half = HEAD_DIM // 2
    inv_freq = ROPE_THETA ** (-jnp.arange(half, dtype=F32) / half)
    ang = pos.astype(F32)[:, None] * inv_freq[None, :]
    cos, sin = jnp.cos(ang), jnp.sin(ang)
    zero = jnp.zeros_like(sin)
    reps = LANES // HEAD_DIM
    cos_t = jnp.tile(jnp.concatenate([cos, cos], axis=1), (1, reps))
    sin_a = jnp.tile(jnp.concatenate([-sin, zero], axis=1), (1, reps))
    sin_b = jnp.tile(jnp.concatenate([zero, sin], axis=1), (1, reps))
    return cos_t, sin_a, sin_b


def _row_tile(t, target):
    tm = min(t, target)
    while t % tm:
        tm //= 2
    return tm


def kernel(x_prompt, x_sample, c_prompt, c_sample, cache_sb_k, cache_sb_v, cache_moba_k, cache_moba_v, state_ffn_conv, page_table, w_ada, b_ada, g_attn, w_in, g_sb_out, g_moba_out, w_out, g_ffn, w_up, conv_w, conv_b, w_down, g_final):
    batch, seq, d = x_prompt.shape
    dec_b, dec_seq, _ = x_sample.shape
    depth = w_ada.shape[0]
    n_pages, page = page_table.shape[1], cache_sb_k.shape[2]
    heads = cache_sb_k.shape[3]
    w = heads * HEAD_DIM
    d_ff = w_down.shape[1]
    past_len = n_pages * page
    assert dec_seq == 1 and d == 2 * w and w_in.shape[2] == 6 * w
    assert seq % MOBA_BLOCK == 0 and past_len % MOBA_BLOCK == 0 and MOBA_BLOCK % page == 0
    assert past_len // MOBA_BLOCK >= MOBA_TOPK and d_ff % FFN_CHUNK == 0

    n_c = batch + dec_b
    c_all = jnp.concatenate([c_prompt, c_sample], axis=0)
    c_all = jnp.pad(c_all, ((0, -n_c % SUBLANES), (0, 0)))
    mod = _ada(c_all, w_ada, b_ada)

    sb_k_t, sb_v_t, mb_k_t, mb_v_t = (jnp.transpose(c, (0, 1, 3, 4, 2)) for c in
                                      (cache_sb_k, cache_sb_v, cache_moba_k, cache_moba_v))
    rope_p = _rope_tables(jnp.arange(seq))
    rope_s = _rope_tables(jnp.full((dec_b,), past_len))
    nb = seq // MOBA_BLOCK
    nbp = -(-nb // LANES) * LANES
    tm_p = _row_tile(seq, 256)
    xs = x_sample.reshape(dec_b, d)
    xps = [x_prompt[b] for b in range(batch)]
    zeros_prev = jnp.zeros((SUBLANES, 2 * d_ff), F32)

    rows_p = [[] for _ in range(4)]
    rows_s = [[] for _ in range(4)]
    conv_p, conv_s = [], []
    for l in range(depth):
        wl = w_in[l]
        wqk = jnp.concatenate([wl[:, 3 * w:4 * w], wl[:, 4 * w:5 * w]], axis=1)
        wqk_hi = wqk.astype(BF16)
        wqk_lo = (wqk - wqk_hi.astype(F32)).astype(BF16)
        wts = (wl[:, :3 * w].astype(BF16), wqk_hi, wqk_lo, wl[:, 5 * w:].astype(BF16))
        wo_a, wo_b = w_out[l, :w].astype(BF16), w_out[l, w:].astype(BF16)
        wup, wdn = w_up[l].astype(BF16), w_down[l].astype(BF16)
        cw, cb = conv_w[l], conv_b[l].reshape(1, -1)
        g_a, g_f = g_attn[l].reshape(1, d), g_ffn[l].reshape(1, d)
        g_sb, g_mb = g_sb_out[l].reshape(1, w), g_moba_out[l].reshape(1, w)

        def mods(lo, hi):
            m = mod[l, lo:hi]
            return [m[:, k * d:(k + 1) * d] for k in range(6)]

        layer_rows = [[] for _ in range(4)]
        layer_conv = []
        for b in range(batch):
            sh1, sc1, ga1, sh2, sc2, ga2 = mods(b, b + 1)
            (k_sb, v_sb, k_mb, v_mb, q_mbf, q_sb16, k_sb16, v_sb16, q_mb16, k_mb16,
             v_mb16) = _proj(xps[b], sc1, sh1, g_a, wts, rope_p, tm_p)
            o_sb = _sb_prompt(q_sb16, k_sb16, v_sb16, g_sb, MOBA_BLOCK)
            kmean = jnp.pad(_kmean(k_mb), ((0, nbp - nb), (0, 0)))
            o_mb = _moba_prompt(q_mbf, q_mb16, k_mb16, v_mb16, kmean, g_mb)
            x_mid, h2 = _outproj(o_sb, o_mb, wo_a, wo_b, xps[b], ga1, g_f, sc2, sh2, tm_p)
            xps[b], state8 = _ffn_prompt(h2, wup, cw, cb, wdn, x_mid, ga2, zeros_prev, tm_p)
            for lst, a in zip(layer_rows, (k_sb, v_sb, k_mb, v_mb)):
                lst.append(a.reshape(seq, heads, HEAD_DIM))
            layer_conv.append(state8[SUBLANES - (CONV_W - 1):])
        for lst, parts in zip(rows_p, layer_rows):
            lst.append(jnp.stack(parts))
        conv_p.append(jnp.stack(layer_conv))

        sh1, sc1, ga1, sh2, sc2, ga2 = mods(batch, n_c)
        (k_sb, v_sb, k_mb, v_mb, q_mbf, q_sb16, _, _, _, _, _) = _proj(
            xs, sc1, sh1, g_a, wts, rope_s, dec_b)
        col = lambda a: a.reshape(dec_b, heads, HEAD_DIM, 1)
        lanes = lambda a: jnp.broadcast_to(col(a), (dec_b, heads, HEAD_DIM, page))
        q_sb = q_sb16.astype(F32) * HEAD_DIM ** 0.5
        o_sb = _sb_decode(page_table, lanes(q_sb), sb_k_t, sb_v_t, l, g_sb.reshape(heads, HEAD_DIM, 1))
        qb_mb = lanes(q_mbf)
        sel = _moba_gate(page_table, qb_mb, mb_k_t, l)
        o_mb = _moba_decode(page_table, sel.reshape(-1), qb_mb, col(k_mb), col(v_mb),
                            mb_k_t, mb_v_t, l, g_mb.reshape(heads, HEAD_DIM, 1))
        x_mid, h2 = _outproj(o_sb.reshape(dec_b, w).astype(BF16), o_mb.reshape(dec_b, w).astype(BF16),
                             wo_a, wo_b, xs, ga1, g_f, sc2, sh2, dec_b)
        prev = state_ffn_conv[l]
        xs, u_new = _ffn_decode(h2, wup, cw, cb, wdn, x_mid, ga2, prev[:, 0], prev[:, 1])
        for lst, a in zip(rows_s, (k_sb, v_sb, k_mb, v_mb)):
            lst.append(a.reshape(dec_b, 1, heads, HEAD_DIM))
        conv_s.append(jnp.stack([prev[:, 1], u_new], axis=1))

    g_fin = g_final.reshape(1, d)
    y_prompt = jnp.stack([_final_norm(xp, g_fin, tm_p) for xp in xps])
    y_sample = _final_norm(xs, g_fin, dec_b).reshape(dec_b, 1, d)
    return (y_prompt, y_sample,
            jnp.stack(rows_p[0]), jnp.stack(rows_p[1]), jnp.stack(rows_p[2]), jnp.stack(rows_p[3]),
            jnp.stack(conv_p),
            jnp.stack(rows_s[0]), jnp.stack(rows_s[1]), jnp.stack(rows_s[2]), jnp.stack(rows_s[3]),
            jnp.stack(conv_s))
```

```python
import functools

import jax
import jax.numpy as jnp
from jax import lax
from jax.experimental import pallas as pl
from jax.experimental.pallas import tpu as pltpu

HEAD_DIM = 64
MOBA_BLOCK = 256
MOBA_TOPK = 3
ROPE_THETA = 10000.0
CONV_W = 3
RMS_EPS = 1e-6

LANES = 128
SUBLANES = 8
VMEM_LIMIT_BYTES = 56 * 1024 * 1024
NEG_BIG = -1e30
SB_EXIT = 110.0
MOBA_KEY_CHUNK = 4
DECODE_PAGE_GROUP = 4
GATE_PAGE_GROUP = 8

F32 = jnp.float32
BF16 = jnp.bfloat16
NT_DIMS = (((1,), (1,)), ((), ()))


def _params(*sem, flags=None):
    return pltpu.CompilerParams(dimension_semantics=sem, vmem_limit_bytes=VMEM_LIMIT_BYTES,
                                flags=flags)


def _dot(a, b):
    return jnp.dot(a, b, preferred_element_type=F32)


def _dot_nt(a, b):
    return lax.dot_general(a, b, NT_DIMS, preferred_element_type=F32)


def _split(x):
    hi = x.astype(BF16)
    lo = (x - hi.astype(F32)).astype(BF16)
    return hi, lo


def _rms(x):
    return x * lax.rsqrt(jnp.mean(x * x, axis=-1, keepdims=True) + RMS_EPS)


def _softplus(z):
    return jnp.maximum(z, 0.0) + jnp.log(1.0 + jnp.exp(-jnp.abs(z)))


def _row_spec(rows, tm, width):
    if rows == 1:
        return pl.BlockSpec((1, width), lambda i: (0, 0))
    return pl.BlockSpec((tm, width), lambda i: (i, 0))


def _const_spec(shape):
    return pl.BlockSpec(shape, lambda *_: (0,) * len(shape), pipeline_mode=pl.Buffered(1))


def _ada_kernel(c_ref, w_ref, b_ref, o_ref):
    c = c_ref[...]
    s = c * (1.0 / (1.0 + jnp.exp(-c)))
    o_ref[0] = jnp.dot(s, w_ref[0], preferred_element_type=F32,
                       precision=lax.Precision.HIGHEST) + b_ref[0]


def _ada(c_all, w_ada, b_ada):
    depth, d, n = w_ada.shape
    r = c_all.shape[0]
    tn = 1536 if n % 1536 == 0 else n
    return pl.pallas_call(
        _ada_kernel,
        grid=(depth, n // tn),
        in_specs=[pl.BlockSpec((r, d), lambda l, j: (0, 0)),
                  pl.BlockSpec((1, d, tn), lambda l, j: (l, 0, j)),
                  pl.BlockSpec((1, 1, tn), lambda l, j: (l, 0, j))],
        out_specs=pl.BlockSpec((1, r, tn), lambda l, j: (l, 0, j)),
        out_shape=jax.ShapeDtypeStruct((depth, r, n), F32),
        compiler_params=_params("arbitrary", "arbitrary"),
        name="ada_mod",
    )(c_all, w_ada, b_ada.reshape(depth, 1, n))


def _rope(x, cos, sin_a, sin_b):
    outs = []
    for c in range(x.shape[1] // LANES):
        xc = x[:, c * LANES:(c + 1) * LANES]
        outs.append(xc * cos + pltpu.roll(xc, LANES - HEAD_DIM // 2, 1) * sin_a
                    + pltpu.roll(xc, HEAD_DIM // 2, 1) * sin_b)
    return jnp.concatenate(outs, axis=1)


def _proj_kernel(x_ref, sc_ref, sh_ref, g_ref, wsb_ref, wqk_hi_ref, wqk_lo_ref, wmv_ref,
                 cos_ref, sa_ref, sb_ref,
                 ksb_ref, vsb_ref, kmb_ref, vmb_ref, qmbf_ref,
                 qsb16_ref, ksb16_ref, vsb16_ref, qmb16_ref, kmb16_ref, vmb16_ref, *, w):
    h = _rms(x_ref[...]) * g_ref[...] * (1.0 + sc_ref[...]) + sh_ref[...]
    h_hi, h_lo = _split(h)
    scale = HEAD_DIM ** -0.5

    sb = _dot(h_hi, wsb_ref[...])
    q_sb, k_sb, v_sb = sb[:, :w], sb[:, w:2 * w], sb[:, 2 * w:]
    ksb_ref[...] = k_sb
    vsb_ref[...] = v_sb
    qsb16_ref[...] = (q_sb * scale).astype(BF16)
    ksb16_ref[...] = k_sb.astype(BF16)
    vsb16_ref[...] = v_sb.astype(BF16)

    w_hi = wqk_hi_ref[...]
    qk = _dot(h_hi, w_hi) + _dot(h_hi, wqk_lo_ref[...]) + _dot(h_lo, w_hi)
    cos, sa, sb_t = cos_ref[...], sa_ref[...], sb_ref[...]
    q_mb = _rope(qk[:, :w], cos, sa, sb_t)
    k_mb = _rope(qk[:, w:], cos, sa, sb_t)
    v_mb = _dot(h_hi, wmv_ref[...])
    kmb_ref[...] = k_mb
    vmb_ref[...] = v_mb
    qmbf_ref[...] = q_mb
    qmb16_ref[...] = (q_mb * scale).astype(BF16)
    kmb16_ref[...] = k_mb.astype(BF16)
    vmb16_ref[...] = v_mb.astype(BF16)


def _proj(x, sc, sh, g, wts, rope_tabs, tm):
    t, d = x.shape
    wsb, wqk_hi, wqk_lo, wmv = wts
    w = wmv.shape[1]
    cos, sa, sb = rope_tabs
    row = lambda width: pl.BlockSpec((tm, width), lambda i: (i, 0))
    f32_out = jax.ShapeDtypeStruct((t, w), F32)
    b16_out = jax.ShapeDtypeStruct((t, w), BF16)
    return pl.pallas_call(
        functools.partial(_proj_kernel, w=w),
        grid=(t // tm,),
        in_specs=[row(d), _row_spec(sc.shape[0], tm, d), _row_spec(sh.shape[0], tm, d),
                  _const_spec((1, d)), _const_spec(wsb.shape), _const_spec(wqk_hi.shape),
                  _const_spec(wqk_lo.shape), _const_spec(wmv.shape),
                  row(LANES), row(LANES), row(LANES)],
        out_specs=[row(w)] * 11,
        out_shape=[f32_out] * 5 + [b16_out] * 6,
        compiler_params=_params("arbitrary"),
        name="qkv_proj",
    )(x, sc, sh, g, wsb, wqk_hi, wqk_lo, wmv, cos, sa, sb)


def _head_norm_pair(o, first, g):
    sq = o * o
    ms_a = jnp.sum(jnp.where(first, sq, 0.0), axis=1, keepdims=True) * (1.0 / HEAD_DIM)
    ms_b = jnp.sum(jnp.where(first, 0.0, sq), axis=1, keepdims=True) * (1.0 / HEAD_DIM)
    r = jnp.where(first, lax.rsqrt(ms_a + RMS_EPS), lax.rsqrt(ms_b + RMS_EPS))
    return o * r * g


def _sb_prompt_kernel(q_ref, k_ref, v_ref, g_ref, o_ref, *, tq):
    i = pl.program_id(1)
    q = q_ref[...]
    lane = lax.broadcasted_iota(jnp.int32, (tq, LANES), 1)
    first = lane < HEAD_DIM
    zero = jnp.zeros_like(q)
    qs = (jnp.where(first, q, zero), jnp.where(first, zero, q))
    r_i = lax.broadcasted_iota(jnp.int32, (tq, tq), 0)
    c_i = lax.broadcasted_iota(jnp.int32, (tq, tq), 1)
    neg_upper = jnp.where(r_i > c_i, -1.0, 0.0).astype(BF16)
    causal = c_i < r_i

    def tile(j, state, diag):
        start = pl.multiple_of(j * tq, tq)
        kt = k_ref[pl.ds(start, tq), :]
        vt = v_ref[pl.ds(start, tq), :]
        out = []
        for a in range(2):
            keep_sum, acc = state[2 * a], state[2 * a + 1]
            z = _dot_nt(qs[a], kt)
            sp = _softplus(z)
            spm = jnp.where(causal, sp, 0.0) if diag else sp
            hi, lo = _split(spm)
            between = _dot(hi, neg_upper) + _dot(lo, neg_upper)
            wgt = jnp.exp(z - sp + between - keep_sum)
            if diag:
                wgt = jnp.where(causal, wgt, 0.0)
            acc = acc + _dot(wgt.astype(BF16), vt)
            keep_sum = keep_sum + jnp.sum(spm, axis=1, keepdims=True)
            out += [keep_sum, acc]
        return tuple(out)

    def alive(st):
        return jnp.min(jnp.minimum(st[0], st[2])) < SB_EXIT

    def step(carry):
        st = tile(i - 1 - carry[0], carry[2:], False)
        return (carry[0] + 1, alive(st)) + st

    zc = jnp.zeros((tq, 1), F32)
    za = jnp.zeros((tq, LANES), F32)
    state = tile(i, (zc, za, zc, za), True)
    carry = lax.while_loop(lambda c: jnp.logical_and(c[0] < i, c[1]), step,
                           (jnp.int32(0), alive(state)) + state)
    state = carry[2:]
    o = jnp.where(first, state[1], state[3])
    o_ref[...] = _head_norm_pair(o, first, g_ref[...]).astype(o_ref.dtype)


def _sb_prompt(q16, k16, v16, g, tq):
    t, w = q16.shape
    return pl.pallas_call(
        functools.partial(_sb_prompt_kernel, tq=tq),
        grid=(w // LANES, t // tq),
        in_specs=[pl.BlockSpec((tq, LANES), lambda p, i: (i, p)),
                  pl.BlockSpec((t, LANES), lambda p, i: (0, p)),
                  pl.BlockSpec((t, LANES), lambda p, i: (0, p)),
                  pl.BlockSpec((1, LANES), lambda p, i: (0, p))],
        out_specs=pl.BlockSpec((tq, LANES), lambda p, i: (i, p)),
        out_shape=jax.ShapeDtypeStruct((t, w), BF16),
        compiler_params=_params("arbitrary", "arbitrary"),
        name="sb_prompt_attn",
    )(q16, k16, v16, g)


def _kmean_kernel(k_ref, o_ref):
    o_ref[0] = jnp.mean(k_ref[...], axis=0, keepdims=True)


def _kmean(k_mb):
    t, w = k_mb.shape
    nb = t // MOBA_BLOCK
    return pl.pallas_call(
        _kmean_kernel,
        grid=(nb,),
        in_specs=[pl.BlockSpec((MOBA_BLOCK, w), lambda n: (n, 0))],
        out_specs=pl.BlockSpec((1, 1, w), lambda n: (n, 0, 0)),
        out_shape=jax.ShapeDtypeStruct((nb, 1, w), F32),
        compiler_params=_params("arbitrary"),
        name="moba_kmean",
    )(k_mb).reshape(nb, w)


def _moba_prompt_kernel(qf_ref, q_ref, k_ref, vt_ref, km_ref, g_ref, o_ref, kaug_ref, s0_ref, s1_ref,
                        *, tq, nbp, kc):
    i = pl.program_id(1)
    q = q_ref[...]
    qf = qf_ref[...]
    km_hi, km_lo = _split(km_ref[...])
    lane = lax.broadcasted_iota(jnp.int32, (tq, LANES), 1)
    first = lane < HEAD_DIM
    col = lax.broadcasted_iota(jnp.int32, (tq, nbp), 1)
    past = col < i
    chunk = kc * tq

    @pl.when(i == 0)
    def _():
        k_lane = lax.broadcasted_iota(jnp.int32, (chunk, LANES), 1)
        k_blk = lax.broadcasted_iota(jnp.int32, (chunk, LANES), 0) // tq
        k_first = k_lane < HEAD_DIM

        def fill(c, carry):
            rows = pl.ds(pl.multiple_of(c * chunk, chunk), chunk)
            kt = k_ref[rows, :]
            blk = k_blk + c * kc
            hot_a = jnp.where(k_lane - HEAD_DIM == blk, 1.0, 0.0).astype(BF16)
            hot_b = jnp.where(k_lane == blk, 1.0, 0.0).astype(BF16)
            kaug_ref[0, rows, :] = jnp.where(k_first, kt, hot_a)
            kaug_ref[1, rows, :] = jnp.where(k_first, hot_b, kt)
            return carry

        lax.fori_loop(0, k_ref.shape[0] // chunk, fill, 0)

    key_i = lax.broadcasted_iota(jnp.int32, (tq, tq), 0)
    qry_i = lax.broadcasted_iota(jnp.int32, (tq, tq), 1)
    own_ok = key_i <= qry_i
    own_rows = pl.ds(pl.multiple_of(i * tq, tq), tq)
    vt_own = vt_ref[:, own_rows]

    q_aug, state = [], []
    for a in range(2):
        mask = first if a == 0 else jnp.logical_not(first)
        qf_hi, qf_lo = _split(jnp.where(mask, qf, 0.0))
        gate = _dot_nt(qf_hi, km_hi) + _dot_nt(qf_hi, km_lo) + _dot_nt(qf_lo, km_hi)
        gate = jnp.where(past, gate, -jnp.inf)
        bias = jnp.full((tq, nbp), NEG_BIG, F32)
        for _ in range(MOBA_TOPK):
            best = jnp.max(gate, axis=1, keepdims=True)
            idx = jnp.min(jnp.where(gate == best, col, nbp), axis=1, keepdims=True)
            hit = col == idx
            bias = jnp.where(jnp.logical_and(hit, past), 0.0, bias)
            gate = jnp.where(hit, -jnp.inf, gate)
        if a == 0:
            bias = pltpu.roll(bias, HEAD_DIM, 1)
        q_aug.append(jnp.where(mask, q, bias.astype(BF16)))
        qa = jnp.where(mask, q, jnp.zeros_like(q))
        s = jnp.where(own_ok, _dot_nt(kaug_ref[a, own_rows, :], qa), -jnp.inf)
        m = jnp.max(s, axis=0, keepdims=True)
        p = jnp.exp(s - m)
        state += [m, jnp.sum(p, axis=0, keepdims=True), _dot(vt_own, p.astype(BF16))]

    n_chunks = (i + kc - 1) // kc
    last_chunk = k_ref.shape[0] // chunk - 1

    def chunk_rows(c):
        return pl.ds(pl.multiple_of(jnp.minimum(c, last_chunk) * chunk, chunk), chunk)

    def score(c, dst):
        rows = chunk_rows(c)
        for a in range(2):
            dst[a] = _dot_nt(kaug_ref[a, rows, :], q_aug[a])

    def absorb(c, src, st):
        vt = vt_ref[:, chunk_rows(c)]
        out = []
        for a in range(2):
            m, l, acc = st[3 * a], st[3 * a + 1], st[3 * a + 2]
            s = src[a]
            m_new = jnp.maximum(m, jnp.max(s, axis=0, keepdims=True))
            alpha = jnp.exp(m - m_new)
            p = jnp.exp(s - m_new)
            l = alpha * l + jnp.sum(p, axis=0, keepdims=True)
            acc = alpha * acc + _dot(vt, p.astype(BF16))
            out += [m_new, l, acc]
        return tuple(out)

    def body(t, st):
        c = 2 * t
        score(c + 1, s1_ref)
        st = absorb(c, s0_ref, st)
        score(c + 2, s0_ref)
        return absorb(c + 1, s1_ref, st)

    score(0, s0_ref)
    st = lax.fori_loop(0, (n_chunks + 1) // 2, body, tuple(state))
    sub = lax.broadcasted_iota(jnp.int32, (LANES, tq), 0)
    top = sub < HEAD_DIM
    o = jnp.where(top, st[2] / st[1], st[5] / st[4])
    sq = o * o
    ms_a = jnp.sum(jnp.where(top, sq, 0.0), axis=0, keepdims=True) * (1.0 / HEAD_DIM)
    ms_b = jnp.sum(jnp.where(top, 0.0, sq), axis=0, keepdims=True) * (1.0 / HEAD_DIM)
    r = jnp.where(top, lax.rsqrt(ms_a + RMS_EPS), lax.rsqrt(ms_b + RMS_EPS))
    o_ref[...] = jnp.transpose(o * r * g_ref[...]).astype(o_ref.dtype)


def _moba_prompt(qf, q16, k16, v16_t, kmean_p, g_col):
    t, w = q16.shape
    tq = MOBA_BLOCK
    nbp = kmean_p.shape[0]
    assert nbp == LANES and t // tq <= HEAD_DIM
    kc = MOBA_KEY_CHUNK
    while (t // tq) % kc:
        kc //= 2
    return pl.pallas_call(
        functools.partial(_moba_prompt_kernel, tq=tq, nbp=nbp, kc=kc),
        grid=(w // LANES, t // tq),
        in_specs=[pl.BlockSpec((tq, LANES), lambda p, i: (i, p)),
                  pl.BlockSpec((tq, LANES), lambda p, i: (i, p)),
                  pl.BlockSpec((t, LANES), lambda p, i: (0, p)),
                  pl.BlockSpec((LANES, t), lambda p, i: (p, 0)),
                  pl.BlockSpec((nbp, LANES), lambda p, i: (0, p)),
                  pl.BlockSpec((LANES, 1), lambda p, i: (p, 0))],
        out_specs=pl.BlockSpec((tq, LANES), lambda p, i: (i, p)),
        out_shape=jax.ShapeDtypeStruct((t, w), BF16),
        scratch_shapes=[pltpu.VMEM((2, t, LANES), BF16),
                        pltpu.VMEM((2, kc * tq, tq), F32), pltpu.VMEM((2, kc * tq, tq), F32)],
        compiler_params=_params("arbitrary", "arbitrary"),
        name="moba_prompt_attn",
    )(qf, q16, k16, v16_t, kmean_p, g_col)


def _outproj_kernel(osb_ref, omb_ref, wa_ref, wb_ref, x_ref, ga_ref, g_ref, sc_ref, sh_ref,
                    xo_ref, h_ref):
    mix = _dot(osb_ref[...], wa_ref[...]) + _dot(omb_ref[...], wb_ref[...])
    x = x_ref[...] + ga_ref[...] * mix
    xo_ref[...] = x
    h_ref[...] = (_rms(x) * g_ref[...] * (1.0 + sc_ref[...]) + sh_ref[...]).astype(h_ref.dtype)


def _outproj(o_sb, o_mb, w_a, w_b, x, ga, g, sc, sh, tm):
    t, d = x.shape
    w = o_sb.shape[1]
    row = lambda width: pl.BlockSpec((tm, width), lambda i: (i, 0))
    mod = lambda a: _row_spec(a.shape[0], tm, d)
    return pl.pallas_call(
        _outproj_kernel,
        grid=(t // tm,),
        in_specs=[row(w), row(w), _const_spec(w_a.shape), _const_spec(w_b.shape), row(d),
                  mod(ga), _const_spec((1, d)), mod(sc), mod(sh)],
        out_specs=[row(d), row(d)],
        out_shape=[jax.ShapeDtypeStruct((t, d), F32), jax.ShapeDtypeStruct((t, d), BF16)],
        compiler_params=_params("arbitrary"),
        name="out_proj",
    )(o_sb, o_mb, w_a, w_b, x, ga, g, sc, sh)


FFN_CHUNK = 256


def _silu_gate(val, gate):
    return val * gate * (1.0 / (1.0 + jnp.exp(-gate)))


def _ffn_prompt_kernel(h_ref, wup_ref, cw_ref, cb_ref, wdn_ref, x_ref, ga_ref, prev_ref,
                       xo_ref, state_ref, carry_ref, *, d_ff, tm):
    i = pl.program_id(0)

    @pl.when(i == 0)
    def _():
        carry_ref[...] = prev_ref[...]

    h = h_ref[...]
    row = lax.broadcasted_iota(jnp.int32, (tm, FFN_CHUNK), 0)

    def conv(off):
        cols = pl.ds(off, FFN_CHUNK)
        u = _dot(h, wup_ref[:, cols])
        prev = carry_ref[:, cols]
        p1, p2 = prev[SUBLANES - 1:SUBLANES, :], prev[SUBLANES - 2:SUBLANES - 1, :]
        u1 = jnp.where(row == 0, p1, pltpu.roll(u, 1, 0))
        u2 = jnp.where(row == 0, p2, jnp.where(row == 1, p1, pltpu.roll(u, 2, 0)))
        carry_ref[:, cols] = u[tm - SUBLANES:, :]
        cw = cw_ref[:, cols]
        return cb_ref[:, cols] + u2 * cw[0:1, :] + u1 * cw[1:2, :] + u * cw[2:3, :]

    acc = jnp.zeros((tm, xo_ref.shape[1]), F32)
    for c in range(d_ff // FFN_CHUNK):
        off = c * FFN_CHUNK
        act = _silu_gate(conv(off), conv(d_ff + off))
        acc = acc + _dot(act.astype(BF16), wdn_ref[pl.ds(off, FFN_CHUNK), :])
    xo_ref[...] = x_ref[...] + ga_ref[...] * acc

    @pl.when(i == pl.num_programs(0) - 1)
    def _():
        state_ref[...] = carry_ref[...]


def _ffn_prompt(h, w_up, conv_w, conv_b, w_down, x, ga, prev8, tm):
    t, d = x.shape
    d_ff = w_down.shape[0]
    row = lambda width: pl.BlockSpec((tm, width), lambda i: (i, 0))
    return pl.pallas_call(
        functools.partial(_ffn_prompt_kernel, d_ff=d_ff, tm=tm),
        grid=(t // tm,),
        in_specs=[row(d), _const_spec(w_up.shape), _const_spec(conv_w.shape),
                  _const_spec(conv_b.shape), _const_spec(w_down.shape), row(d),
                  _row_spec(ga.shape[0], tm, d), _const_spec(prev8.shape)],
        out_specs=[row(d), pl.BlockSpec(prev8.shape, lambda i: (0, 0))],
        out_shape=[jax.ShapeDtypeStruct((t, d), F32), jax.ShapeDtypeStruct(prev8.shape, F32)],
        scratch_shapes=[pltpu.VMEM(prev8.shape, F32)],
        compiler_params=_params("arbitrary"),
        name="conv_ffn_prompt",
    )(h, w_up, conv_w, conv_b, w_down, x, ga, prev8)


def _ffn_decode_kernel(h_ref, wup_ref, cw_ref, cb_ref, wdn_ref, x_ref, ga_ref, p0_ref, p1_ref,
                       xo_ref, u_ref, *, d_ff):
    h = h_ref[...]

    def conv(off):
        cols = pl.ds(off, FFN_CHUNK)
        u = _dot(h, wup_ref[:, cols])
        u_ref[:, cols] = u
        cw = cw_ref[:, cols]
        return (cb_ref[:, cols] + p0_ref[:, cols] * cw[0:1, :] + p1_ref[:, cols] * cw[1:2, :]
                + u * cw[2:3, :])

    acc = jnp.zeros(xo_ref.shape, F32)
    for c in range(d_ff // FFN_CHUNK):
        off = c * FFN_CHUNK
        act = _silu_gate(conv(off), conv(d_ff + off))
        acc = acc + _dot(act.astype(BF16), wdn_ref[pl.ds(off, FFN_CHUNK), :])
    xo_ref[...] = x_ref[...] + ga_ref[...] * acc


def _ffn_decode(h, w_up, conv_w, conv_b, w_down, x, ga, prev0, prev1):
    b, d = x.shape
    d_ff = w_down.shape[0]
    full = lambda a: pl.BlockSpec(a.shape, lambda i: (0,) * a.ndim)
    args = (h, w_up, conv_w, conv_b, w_down, x, ga, prev0, prev1)
    return pl.pallas_call(
        functools.partial(_ffn_decode_kernel, d_ff=d_ff),
        grid=(1,),
        in_specs=[full(a) for a in args],
        out_specs=[pl.BlockSpec((b, d), lambda i: (0, 0)),
                   pl.BlockSpec((b, 2 * d_ff), lambda i: (0, 0))],
        out_shape=[jax.ShapeDtypeStruct((b, d), F32), jax.ShapeDtypeStruct((b, 2 * d_ff), F32)],
        compiler_params=_params("arbitrary"),
        name="conv_ffn_decode",
    )(*args)


def _final_norm_kernel(x_ref, g_ref, o_ref):
    o_ref[...] = _rms(x_ref[...]) * g_ref[...]


def _final_norm(x, g, tm):
    t, d = x.shape
    return pl.pallas_call(
        _final_norm_kernel,
        grid=(t // tm,),
        in_specs=[pl.BlockSpec((tm, d), lambda i: (i, 0)), pl.BlockSpec((1, d), lambda i: (0, 0))],
        out_specs=pl.BlockSpec((tm, d), lambda i: (i, 0)),
        out_shape=jax.ShapeDtypeStruct((t, d), F32),
        compiler_params=_params("arbitrary"),
        name="final_norm",
    )(x, g)


def _page_scores(qb, k_page, heads):
    sub = lax.broadcasted_iota(jnp.int32, (heads, k_page.shape[-1]), 0)
    z = jnp.zeros(sub.shape, F32)
    for h in range(heads):
        row = jnp.sum(qb[h] * k_page[h], axis=0, keepdims=True)
        z = jnp.where(sub == h, row, z)
    return z


def _sb_decode_kernel(pt_ref, qb_ref, k_hbm, v_hbm, g_ref, o_ref, kbuf, vbuf, acc_ref, sem,
                      *, layer, n_pages, group, heads):
    b = pl.program_id(0)
    n_chunks = n_pages // group
    page = kbuf.shape[-1]
    lane = lax.broadcasted_iota(jnp.int32, (heads, page), 1)
    qb = qb_ref.at[0]

    def copies(c, slot):
        out = []
        for g in range(group):
            pg = pt_ref[b, c * group + g]
            out.append(pltpu.make_async_copy(k_hbm.at[layer, pg], kbuf.at[slot, g], sem.at[0, slot]))
            out.append(pltpu.make_async_copy(v_hbm.at[layer, pg], vbuf.at[slot, g], sem.at[1, slot]))
        return out

    def start(c, slot):
        for cp in copies(c, slot):
            cp.start()

    def wait(c, slot):
        for cp in copies(c, slot):
            cp.wait()

    def page_update(slot, g, keep):
        z = _page_scores(qb, kbuf.at[slot, g], heads) * HEAD_DIM ** -0.5
        sp = _softplus(z)
        suffix = sp
        shift = 1
        while shift < page:
            moved = pltpu.roll(suffix, page - shift, 1)
            suffix = suffix + jnp.where(lane + shift < page, moved, 0.0)
            shift *= 2
        wgt = jnp.exp(z - sp - (suffix - sp) - keep)
        for h in range(heads):
            acc_ref[h] += wgt[h:h + 1, :] * vbuf[slot, g, h]
        return keep + jnp.sum(sp, axis=1, keepdims=True)

    def step(carry):
        c, _, keep = carry
        slot = c % 2

        @pl.when(c > 0)
        def _():
            start(c - 1, 1 - slot)

        wait(c, slot)
        for g in reversed(range(group)):
            keep = page_update(slot, g, keep)
        return c - 1, jnp.min(keep) < SB_EXIT, keep

    acc_ref[...] = jnp.zeros_like(acc_ref)
    last = n_chunks - 1
    start(last, last % 2)
    c_end, _, _ = lax.while_loop(lambda cr: jnp.logical_and(cr[0] >= 0, cr[1]), step,
                                 (jnp.int32(last), jnp.bool_(True), jnp.zeros((heads, 1), F32)))

    @pl.when(c_end >= 0)
    def _():
        wait(c_end, c_end % 2)

    for h in range(heads):
        o = jnp.sum(acc_ref[h], axis=1, keepdims=True)
        o_ref[0, h] = o * lax.rsqrt(jnp.mean(o * o, axis=0, keepdims=True) + RMS_EPS) * g_ref[h]


def _sb_decode(page_table, qb, cache_k, cache_v, layer, g):
    b, heads, hd, page = qb.shape
    n_pages = page_table.shape[1]
    group = DECODE_PAGE_GROUP
    while n_pages % group:
        group //= 2
    grid_spec = pltpu.PrefetchScalarGridSpec(
        num_scalar_prefetch=1,
        grid=(b,),
        in_specs=[pl.BlockSpec((1, heads, hd, page), lambda i, pt: (i, 0, 0, 0)),
                  pl.BlockSpec(memory_space=pl.ANY), pl.BlockSpec(memory_space=pl.ANY),
                  pl.BlockSpec((heads, hd, 1), lambda i, pt: (0, 0, 0))],
        out_specs=pl.BlockSpec((1, heads, hd, 1), lambda i, pt: (i, 0, 0, 0)),
        scratch_shapes=[pltpu.VMEM((2, group, heads, hd, page), F32),
                        pltpu.VMEM((2, group, heads, hd, page), F32),
                        pltpu.VMEM((heads, hd, page), F32),
                        pltpu.SemaphoreType.DMA((2, 2))],
    )
    return pl.pallas_call(
        functools.partial(_sb_decode_kernel, layer=layer, n_pages=n_pages, group=group, heads=heads),
        grid_spec=grid_spec,
        out_shape=jax.ShapeDtypeStruct((b, heads, hd, 1), F32),
        compiler_params=_params("arbitrary"),
        name="sb_decode_attn",
    )(page_table, qb, cache_k, cache_v, g)


def _moba_gate_kernel(pt_ref, qb_ref, *refs, group, ppb, heads):
    del pt_ref
    k_refs = refs[:group]
    sel_ref, gate_ref = refs[group], refs[group + 1]
    s = pl.program_id(1)
    page = k_refs[0].shape[-1]
    lane = lax.broadcasted_iota(jnp.int32, (heads, page), 1)
    qb = qb_ref.at[0]

    @pl.when(s == 0)
    def _():
        gate_ref[...] = jnp.full(gate_ref.shape, -jnp.inf, F32)

    gates = gate_ref[...]
    for blk in range(group // ppb):
        z = _page_scores(qb, k_refs[blk * ppb], heads)
        for o in range(1, ppb):
            z = z + _page_scores(qb, k_refs[blk * ppb + o], heads)
        gate = jnp.sum(z, axis=1, keepdims=True) * (1.0 / MOBA_BLOCK)
        gates = jnp.where(lane == s * (group // ppb) + blk, gate, gates)
    gate_ref[...] = gates

    @pl.when(s == pl.num_programs(1) - 1)
    def _():
        gates = gate_ref[...]
        for r in range(MOBA_TOPK):
            best = jnp.max(gates, axis=1, keepdims=True)
            idx = jnp.min(jnp.where(gates == best, lane, page), axis=1, keepdims=True)
            sel_ref[0, r] = jnp.broadcast_to(idx, (heads, page))
            gates = jnp.where(lane == idx, -jnp.inf, gates)


def _moba_gate(page_table, qb, cache_k, layer):
    b, heads, hd, page = qb.shape
    n_pages = page_table.shape[1]
    ppb = MOBA_BLOCK // page
    assert n_pages // ppb <= page
    group = max(ppb, GATE_PAGE_GROUP)
    while n_pages % group:
        group -= ppb
    cache_specs = [pl.BlockSpec((None, None, heads, hd, page),
                                functools.partial(lambda i, s, pt, o: (layer, pt[i, group * s + o], 0, 0, 0), o=o))
                   for o in range(group)]
    grid_spec = pltpu.PrefetchScalarGridSpec(
        num_scalar_prefetch=1,
        grid=(b, n_pages // group),
        in_specs=[pl.BlockSpec((1, heads, hd, page), lambda i, s, pt: (i, 0, 0, 0))] + cache_specs,
        out_specs=pl.BlockSpec((1, MOBA_TOPK, heads, page), lambda i, s, pt: (i, 0, 0, 0)),
        scratch_shapes=[pltpu.VMEM((heads, page), F32)],
    )
    sel = pl.pallas_call(
        functools.partial(_moba_gate_kernel, group=group, ppb=ppb, heads=heads),
        grid_spec=grid_spec,
        out_shape=jax.ShapeDtypeStruct((b, MOBA_TOPK, heads, page), jnp.int32),
        compiler_params=_params("arbitrary", "arbitrary"),
        name="moba_decode_gate",
    )(page_table, qb, *([cache_k] * group))
    return sel[..., 0]


def _moba_decode_kernel(pt_ref, sel_ref, qb_ref, kn_ref, vn_ref, *refs, n_sel):
    del pt_ref, sel_ref
    k_refs, v_refs = refs[:n_sel], refs[n_sel:2 * n_sel]
    g_ref, o_ref = refs[2 * n_sel], refs[2 * n_sel + 1]
    scale = HEAD_DIM ** -0.5
    qb = qb_ref[...]
    scores = [jnp.sum(qb * r[...], axis=0, keepdims=True) * scale for r in k_refs]
    s_own = jnp.sum(qb[:, :1] * kn_ref[...], axis=0, keepdims=True) * scale
    m = s_own
    for s in scores:
        m = jnp.maximum(m, jnp.max(s, axis=1, keepdims=True))
    p_own = jnp.exp(s_own - m)
    l = p_own
    acc = jnp.zeros(qb.shape, F32)
    for s, v_ref in zip(scores, v_refs):
        p = jnp.exp(s - m)
        l = l + jnp.sum(p, axis=1, keepdims=True)
        acc = acc + p * v_ref[...]
    o = (jnp.sum(acc, axis=1, keepdims=True) + p_own * vn_ref[...]) / l
    o_ref[...] = o * lax.rsqrt(jnp.mean(o * o, axis=0, keepdims=True) + RMS_EPS) * g_ref[...]


def _moba_decode(page_table, sel_flat, qb, k_new, v_new, cache_k, cache_v, layer, g):
    b, heads, hd, page = qb.shape
    ppb = MOBA_BLOCK // page
    n_sel = MOBA_TOPK * ppb

    def cache_map(i, h, pt, sel, n):
        blk = sel[(i * MOBA_TOPK + n // ppb) * heads + h]
        return (layer, pt[i, blk * ppb + n % ppb], h, 0, 0)

    cache_specs = [pl.BlockSpec((None, None, None, hd, page), functools.partial(cache_map, n=n))
                   for n in range(n_sel)]
    head_spec = lambda last: pl.BlockSpec((None, None, hd, last), lambda i, h, pt, sel: (i, h, 0, 0))
    grid_spec = pltpu.PrefetchScalarGridSpec(
        num_scalar_prefetch=2,
        grid=(b, heads),
        in_specs=[head_spec(page), head_spec(1), head_spec(1)] + cache_specs + cache_specs
                 + [pl.BlockSpec((None, hd, 1), lambda i, h, pt, sel: (h, 0, 0))],
        out_specs=head_spec(1),
    )
    return pl.pallas_call(
        functools.partial(_moba_decode_kernel, n_sel=n_sel),
        grid_spec=grid_spec,
        out_shape=jax.ShapeDtypeStruct((b, heads, hd, 1), F32),
        compiler_params=_params("arbitrary", "arbitrary"),
        name="moba_decode_attn",
    )(page_table, sel_flat, qb, k_new, v_new, *([cache_k] * n_sel), *([cache_v] * n_sel), g)


def _rope_tables(pos):
    half = HEAD_DIM // 2
    inv_freq = ROPE_THETA ** (-jnp.arange(half, dtype=F32) / half)
    ang = pos.astype(F32)[:, None] * inv_freq[None, :]
    cos, sin = jnp.cos(ang), jnp.sin(ang)
    zero = jnp.zeros_like(sin)
    reps = LANES // HEAD_DIM
    cos_t = jnp.tile(jnp.concatenate([cos, cos], axis=1), (1, reps))
    sin_a = jnp.tile(jnp.concatenate([-sin, zero], axis=1), (1, reps))
    sin_b = jnp.tile(jnp.concatenate([zero, sin], axis=1), (1, reps))
    return cos_t, sin_a, sin_b


def _row_tile(t, target):
    tm = min(t, target)
    while t % tm:
        tm //= 2
    return tm


def kernel(x_prompt, x_sample, c_prompt, c_sample, cache_sb_k, cache_sb_v, cache_moba_k, cache_moba_v, state_ffn_conv, page_table, w_ada, b_ada, g_attn, w_in, g_sb_out, g_moba_out, w_out, g_ffn, w_up, conv_w, conv_b, w_down, g_final):
    batch, seq, d = x_prompt.shape
    dec_b, dec_seq, _ = x_sample.shape
    depth = w_ada.shape[0]
    n_pages, page = page_table.shape[1], cache_sb_k.shape[2]
    heads = cache_sb_k.shape[3]
    w = heads * HEAD_DIM
    d_ff = w_down.shape[1]
    past_len = n_pages * page
    assert dec_seq == 1 and d == 2 * w and w_in.shape[2] == 6 * w
    assert seq % MOBA_BLOCK == 0 and past_len % MOBA_BLOCK == 0 and MOBA_BLOCK % page == 0
    assert past_len // MOBA_BLOCK >= MOBA_TOPK and d_ff % FFN_CHUNK == 0

    n_c = batch + dec_b
    c_all = jnp.concatenate([c_prompt, c_sample], axis=0)
    c_all = jnp.pad(c_all, ((0, -n_c % SUBLANES), (0, 0)))
    mod = _ada(c_all, w_ada, b_ada)

    sb_k_t, sb_v_t, mb_k_t, mb_v_t = (jnp.transpose(c, (0, 1, 3, 4, 2)) for c in
                                      (cache_sb_k, cache_sb_v, cache_moba_k, cache_moba_v))
    rope_p = _rope_tables(jnp.arange(seq))
    rope_s = _rope_tables(jnp.full((dec_b,), past_len))
    nb = seq // MOBA_BLOCK
    nbp = -(-nb // LANES) * LANES
    tm_p = _row_tile(seq, 512)
    xs = x_sample.reshape(dec_b, d)
    xps = [x_prompt[b] for b in range(batch)]
    zeros_prev = jnp.zeros((SUBLANES, 2 * d_ff), F32)

    rows_p = [[] for _ in range(4)]
    rows_s = [[] for _ in range(4)]
    conv_p, conv_s = [], []
    for l in range(depth):
        wl = w_in[l]
        wqk = jnp.concatenate([wl[:, 3 * w:4 * w], wl[:, 4 * w:5 * w]], axis=1)
        wqk_hi = wqk.astype(BF16)
        wqk_lo = (wqk - wqk_hi.astype(F32)).astype(BF16)
        wts = (wl[:, :3 * w].astype(BF16), wqk_hi, wqk_lo, wl[:, 5 * w:].astype(BF16))
        wo_a, wo_b = w_out[l, :w].astype(BF16), w_out[l, w:].astype(BF16)
        wup, wdn = w_up[l].astype(BF16), w_down[l].astype(BF16)
        cw, cb = conv_w[l], conv_b[l].reshape(1, -1)
        g_a, g_f = g_attn[l].reshape(1, d), g_ffn[l].reshape(1, d)
        g_sb, g_mb = g_sb_out[l].reshape(1, w), g_moba_out[l].reshape(1, w)

        def mods(lo, hi):
            m = mod[l, lo:hi]
            return [m[:, k * d:(k + 1) * d] for k in range(6)]

        layer_rows = [[] for _ in range(4)]
        layer_conv = []
        for b in range(batch):
            sh1, sc1, ga1, sh2, sc2, ga2 = mods(b, b + 1)
            (k_sb, v_sb, k_mb, v_mb, q_mbf, q_sb16, k_sb16, v_sb16, q_mb16, k_mb16,
             v_mb16) = _proj(xps[b], sc1, sh1, g_a, wts, rope_p, tm_p)
            o_sb = _sb_prompt(q_sb16, k_sb16, v_sb16, g_sb, MOBA_BLOCK)
            kmean = jnp.pad(_kmean(k_mb), ((0, nbp - nb), (0, 0)))
            o_mb = _moba_prompt(q_mbf, q_mb16, k_mb16, v_mb16.T, kmean, g_mb.reshape(w, 1))
            x_mid, h2 = _outproj(o_sb, o_mb, wo_a, wo_b, xps[b], ga1, g_f, sc2, sh2, tm_p)
            xps[b], state8 = _ffn_prompt(h2, wup, cw, cb, wdn, x_mid, ga2, zeros_prev, tm_p)
            for lst, a in zip(layer_rows, (k_sb, v_sb, k_mb, v_mb)):
                lst.append(a.reshape(seq, heads, HEAD_DIM))
            layer_conv.append(state8[SUBLANES - (CONV_W - 1):])
        for lst, parts in zip(rows_p, layer_rows):
            lst.append(jnp.stack(parts))
        conv_p.append(jnp.stack(layer_conv))

        sh1, sc1, ga1, sh2, sc2, ga2 = mods(batch, n_c)
        (k_sb, v_sb, k_mb, v_mb, q_mbf, q_sb16, _, _, _, _, _) = _proj(
            xs, sc1, sh1, g_a, wts, rope_s, dec_b)
        col = lambda a: a.reshape(dec_b, heads, HEAD_DIM, 1)
        lanes = lambda a: jnp.broadcast_to(col(a), (dec_b, heads, HEAD_DIM, page))
        q_sb = q_sb16.astype(F32) * HEAD_DIM ** 0.5
        o_sb = _sb_decode(page_table, lanes(q_sb), sb_k_t, sb_v_t, l, g_sb.reshape(heads, HEAD_DIM, 1))
        qb_mb = lanes(q_mbf)
        sel = _moba_gate(page_table, qb_mb, mb_k_t, l)
        o_mb = _moba_decode(page_table, sel.reshape(-1), qb_mb, col(k_mb), col(v_mb),
                            mb_k_t, mb_v_t, l, g_mb.reshape(heads, HEAD_DIM, 1))
        x_mid, h2 = _outproj(o_sb.reshape(dec_b, w).astype(BF16), o_mb.reshape(dec_b, w).astype(BF16),
                             wo_a, wo_b, xs, ga1, g_f, sc2, sh2, dec_b)
        prev = state_ffn_conv[l]
        xs, u_new = _ffn_decode(h2, wup, cw, cb, wdn, x_mid, ga2, prev[:, 0], prev[:, 1])
        for lst, a in zip(rows_s, (k_sb, v_sb, k_mb, v_mb)):
            lst.append(a.reshape(dec_b, 1, heads, HEAD_DIM))
        conv_s.append(jnp.stack([prev[:, 1], u_new], axis=1))

    g_fin = g_final.reshape(1, d)
    y_prompt = jnp.stack([_final_norm(xp, g_fin, tm_p) for xp in xps])
    y_sample = _final_norm(xs, g_fin, dec_b).reshape(dec_b, 1, d)
    return (y_prompt, y_sample,
            jnp.stack(rows_p[0]), jnp.stack(rows_p[1]), jnp.stack(rows_p[2]), jnp.stack(rows_p[3]),
            jnp.stack(conv_p),
            jnp.stack(rows_s[0]), jnp.stack(rows_s[1]), jnp.stack(rows_s[2]), jnp.stack(rows_s[3]),
            jnp.stack(conv_s))
```

```python
import functools

import jax
import jax.numpy as jnp
from jax import lax
from jax.experimental import pallas as pl
from jax.experimental.pallas import tpu as pltpu

HEAD_DIM = 64
MOBA_BLOCK = 256
MOBA_TOPK = 3
ROPE_THETA = 10000.0
CONV_W = 3
RMS_EPS = 1e-6

LANES = 128
SUBLANES = 8
VMEM_LIMIT_BYTES = 56 * 1024 * 1024
NEG_BIG = -1e30
SB_EXIT = 110.0
MOBA_KEY_CHUNK = 4
DECODE_PAGE_GROUP = 4
GATE_PAGE_GROUP = 16
MOBA_DECODE_HEADS = 2

F32 = jnp.float32
BF16 = jnp.bfloat16
NT_DIMS = (((1,), (1,)), ((), ()))


def _params(*sem, flags=None):
    return pltpu.CompilerParams(dimension_semantics=sem, vmem_limit_bytes=VMEM_LIMIT_BYTES,
                                flags=flags)


def _dot(a, b):
    return jnp.dot(a, b, preferred_element_type=F32)


def _dot_nt(a, b):
    return lax.dot_general(a, b, NT_DIMS, preferred_element_type=F32)


def _split(x):
    hi = x.astype(BF16)
    lo = (x - hi.astype(F32)).astype(BF16)
    return hi, lo


def _rms(x):
    return x * lax.rsqrt(jnp.mean(x * x, axis=-1, keepdims=True) + RMS_EPS)


def _softplus(z):
    return jnp.maximum(z, 0.0) + jnp.log(1.0 + jnp.exp(-jnp.abs(z)))


def _row_spec(rows, tm, width):
    if rows == 1:
        return pl.BlockSpec((1, width), lambda i: (0, 0))
    return pl.BlockSpec((tm, width), lambda i: (i, 0))


def _const_spec(shape):
    return pl.BlockSpec(shape, lambda *_: (0,) * len(shape), pipeline_mode=pl.Buffered(1))


def _ada_kernel(c_ref, w_ref, b_ref, o_ref):
    c = c_ref[...]
    s = c * (1.0 / (1.0 + jnp.exp(-c)))
    o_ref[0] = jnp.dot(s, w_ref[0], preferred_element_type=F32,
                       precision=lax.Precision.HIGHEST) + b_ref[0]


def _ada(c_all, w_ada, b_ada):
    depth, d, n = w_ada.shape
    r = c_all.shape[0]
    tn = 1536 if n % 1536 == 0 else n
    return pl.pallas_call(
        _ada_kernel,
        grid=(depth, n // tn),
        in_specs=[pl.BlockSpec((r, d), lambda l, j: (0, 0)),
                  pl.BlockSpec((1, d, tn), lambda l, j: (l, 0, j)),
                  pl.BlockSpec((1, 1, tn), lambda l, j: (l, 0, j))],
        out_specs=pl.BlockSpec((1, r, tn), lambda l, j: (l, 0, j)),
        out_shape=jax.ShapeDtypeStruct((depth, r, n), F32),
        compiler_params=_params("arbitrary", "arbitrary"),
        name="ada_mod",
    )(c_all, w_ada, b_ada.reshape(depth, 1, n))


def _rope(x, cos, sin_a, sin_b):
    outs = []
    for c in range(x.shape[1] // LANES):
        xc = x[:, c * LANES:(c + 1) * LANES]
        outs.append(xc * cos + pltpu.roll(xc, LANES - HEAD_DIM // 2, 1) * sin_a
                    + pltpu.roll(xc, HEAD_DIM // 2, 1) * sin_b)
    return jnp.concatenate(outs, axis=1)


def _proj_kernel(x_ref, sc_ref, sh_ref, g_ref, wsb_ref, wqk_hi_ref, wqk_lo_ref, wmv_ref,
                 cos_ref, sa_ref, sb_ref,
                 ksb_ref, vsb_ref, kmb_ref, vmb_ref, qmbf_ref,
                 qsb16_ref, ksb16_ref, vsb16_ref, qmb16_ref, kmb16_ref, vmb16_ref, *, w):
    h = _rms(x_ref[...]) * g_ref[...] * (1.0 + sc_ref[...]) + sh_ref[...]
    h_hi, h_lo = _split(h)
    scale = HEAD_DIM ** -0.5

    sb = _dot(h_hi, wsb_ref[...])
    q_sb, k_sb, v_sb = sb[:, :w], sb[:, w:2 * w], sb[:, 2 * w:]
    ksb_ref[...] = k_sb
    vsb_ref[...] = v_sb
    qsb16_ref[...] = (q_sb * scale).astype(BF16)
    ksb16_ref[...] = k_sb.astype(BF16)
    vsb16_ref[...] = v_sb.astype(BF16)

    w_hi = wqk_hi_ref[...]
    qk = _dot(h_hi, w_hi) + _dot(h_hi, wqk_lo_ref[...]) + _dot(h_lo, w_hi)
    cos, sa, sb_t = cos_ref[...], sa_ref[...], sb_ref[...]
    q_mb = _rope(qk[:, :w], cos, sa, sb_t)
    k_mb = _rope(qk[:, w:], cos, sa, sb_t)
    v_mb = _dot(h_hi, wmv_ref[...])
    kmb_ref[...] = k_mb
    vmb_ref[...] = v_mb
    qmbf_ref[...] = q_mb
    qmb16_ref[...] = (q_mb * scale).astype(BF16)
    kmb16_ref[...] = k_mb.astype(BF16)
    vmb16_ref[...] = v_mb.astype(BF16)


def _proj(x, sc, sh, g, wts, rope_tabs, tm):
    t, d = x.shape
    wsb, wqk_hi, wqk_lo, wmv = wts
    w = wmv.shape[1]
    cos, sa, sb = rope_tabs
    row = lambda width: pl.BlockSpec((tm, width), lambda i: (i, 0))
    f32_out = jax.ShapeDtypeStruct((t, w), F32)
    b16_out = jax.ShapeDtypeStruct((t, w), BF16)
    return pl.pallas_call(
        functools.partial(_proj_kernel, w=w),
        grid=(t // tm,),
        in_specs=[row(d), _row_spec(sc.shape[0], tm, d), _row_spec(sh.shape[0], tm, d),
                  _const_spec((1, d)), _const_spec(wsb.shape), _const_spec(wqk_hi.shape),
                  _const_spec(wqk_lo.shape), _const_spec(wmv.shape),
                  row(LANES), row(LANES), row(LANES)],
        out_specs=[row(w)] * 11,
        out_shape=[f32_out] * 5 + [b16_out] * 6,
        compiler_params=_params("arbitrary"),
        name="qkv_proj",
    )(x, sc, sh, g, wsb, wqk_hi, wqk_lo, wmv, cos, sa, sb)


def _head_norm_pair(o, first, g):
    sq = o * o
    ms_a = jnp.sum(jnp.where(first, sq, 0.0), axis=1, keepdims=True) * (1.0 / HEAD_DIM)
    ms_b = jnp.sum(jnp.where(first, 0.0, sq), axis=1, keepdims=True) * (1.0 / HEAD_DIM)
    r = jnp.where(first, lax.rsqrt(ms_a + RMS_EPS), lax.rsqrt(ms_b + RMS_EPS))
    return o * r * g


def _sb_prompt_kernel(q_ref, k_ref, v_ref, g_ref, o_ref, *, tq):
    i = pl.program_id(1)
    q = q_ref[...]
    lane = lax.broadcasted_iota(jnp.int32, (tq, LANES), 1)
    first = lane < HEAD_DIM
    zero = jnp.zeros_like(q)
    qs = (jnp.where(first, q, zero), jnp.where(first, zero, q))
    r_i = lax.broadcasted_iota(jnp.int32, (tq, tq), 0)
    c_i = lax.broadcasted_iota(jnp.int32, (tq, tq), 1)
    neg_upper = jnp.where(r_i > c_i, -1.0, 0.0).astype(BF16)
    causal = c_i < r_i

    def tile(j, state, diag):
        start = pl.multiple_of(j * tq, tq)
        kt = k_ref[pl.ds(start, tq), :]
        vt = v_ref[pl.ds(start, tq), :]
        out = []
        for a in range(2):
            keep_sum, acc = state[2 * a], state[2 * a + 1]
            z = _dot_nt(qs[a], kt)
            sp = _softplus(z)
            spm = jnp.where(causal, sp, 0.0) if diag else sp
            hi, lo = _split(spm)
            between = _dot(hi, neg_upper) + _dot(lo, neg_upper)
            wgt = jnp.exp(z - sp + between - keep_sum)
            if diag:
                wgt = jnp.where(causal, wgt, 0.0)
            acc = acc + _dot(wgt.astype(BF16), vt)
            keep_sum = keep_sum + jnp.sum(spm, axis=1, keepdims=True)
            out += [keep_sum, acc]
        return tuple(out)

    def alive(st):
        return jnp.min(jnp.minimum(st[0], st[2])) < SB_EXIT

    def step(carry):
        st = tile(i - 1 - carry[0], carry[2:], False)
        return (carry[0] + 1, alive(st)) + st

    zc = jnp.zeros((tq, 1), F32)
    za = jnp.zeros((tq, LANES), F32)
    state = tile(i, (zc, za, zc, za), True)
    carry = lax.while_loop(lambda c: jnp.logical_and(c[0] < i, c[1]), step,
                           (jnp.int32(0), alive(state)) + state)
    state = carry[2:]
    o = jnp.where(first, state[1], state[3])
    o_ref[...] = _head_norm_pair(o, first, g_ref[...]).astype(o_ref.dtype)


def _sb_prompt(q16, k16, v16, g, tq):
    t, w = q16.shape
    return pl.pallas_call(
        functools.partial(_sb_prompt_kernel, tq=tq),
        grid=(w // LANES, t // tq),
        in_specs=[pl.BlockSpec((tq, LANES), lambda p, i: (i, p)),
                  pl.BlockSpec((t, LANES), lambda p, i: (0, p)),
                  pl.BlockSpec((t, LANES), lambda p, i: (0, p)),
                  pl.BlockSpec((1, LANES), lambda p, i: (0, p))],
        out_specs=pl.BlockSpec((tq, LANES), lambda p, i: (i, p)),
        out_shape=jax.ShapeDtypeStruct((t, w), BF16),
        compiler_params=_params("arbitrary", "arbitrary"),
        name="sb_prompt_attn",
    )(q16, k16, v16, g)


def _kmean_kernel(k_ref, o_ref):
    o_ref[0] = jnp.mean(k_ref[...], axis=0, keepdims=True)


def _kmean(k_mb):
    t, w = k_mb.shape
    nb = t // MOBA_BLOCK
    return pl.pallas_call(
        _kmean_kernel,
        grid=(nb,),
        in_specs=[pl.BlockSpec((MOBA_BLOCK, w), lambda n: (n, 0))],
        out_specs=pl.BlockSpec((1, 1, w), lambda n: (n, 0, 0)),
        out_shape=jax.ShapeDtypeStruct((nb, 1, w), F32),
        compiler_params=_params("arbitrary"),
        name="moba_kmean",
    )(k_mb).reshape(nb, w)


def _moba_prompt_kernel(qf_ref, q_ref, k_ref, vt_ref, km_ref, g_ref, o_ref, kaug_ref, s0_ref, s1_ref,
                        *, tq, nbp, kc):
    i = pl.program_id(1)
    q = q_ref[...]
    qf = qf_ref[...]
    km_hi, km_lo = _split(km_ref[...])
    lane = lax.broadcasted_iota(jnp.int32, (tq, LANES), 1)
    first = lane < HEAD_DIM
    blk = lax.broadcasted_iota(jnp.int32, (nbp, tq), 0)
    past = blk < i
    chunk = kc * tq

    @pl.when(i == 0)
    def _():
        k_lane = lax.broadcasted_iota(jnp.int32, (chunk, LANES), 1)
        k_blk = lax.broadcasted_iota(jnp.int32, (chunk, LANES), 0) // tq
        k_first = k_lane < HEAD_DIM

        def fill(c, carry):
            rows = pl.ds(pl.multiple_of(c * chunk, chunk), chunk)
            kt = k_ref[rows, :]
            key_blk = k_blk + c * kc
            hot_a = jnp.where(k_lane - HEAD_DIM == key_blk, 1.0, 0.0).astype(BF16)
            hot_b = jnp.where(k_lane == key_blk, 1.0, 0.0).astype(BF16)
            kaug_ref[0, rows, :] = jnp.where(k_first, kt, hot_a)
            kaug_ref[1, rows, :] = jnp.where(k_first, hot_b, kt)
            return carry

        lax.fori_loop(0, k_ref.shape[0] // chunk, fill, 0)

    key_i = lax.broadcasted_iota(jnp.int32, (tq, tq), 0)
    qry_i = lax.broadcasted_iota(jnp.int32, (tq, tq), 1)
    own_ok = key_i <= qry_i
    own_rows = pl.ds(pl.multiple_of(i * tq, tq), tq)
    vt_own = vt_ref[:, own_rows]

    q_aug, state = [], []
    for a in range(2):
        mask = first if a == 0 else jnp.logical_not(first)
        qf_hi, qf_lo = _split(jnp.where(mask, qf, 0.0))
        gate = _dot_nt(km_hi, qf_hi) + _dot_nt(km_lo, qf_hi) + _dot_nt(km_hi, qf_lo)
        gate = jnp.where(past, gate, -jnp.inf)
        bias = jnp.full((nbp, tq), NEG_BIG, F32)
        for _ in range(MOBA_TOPK):
            best = jnp.max(gate, axis=0, keepdims=True)
            idx = jnp.min(jnp.where(gate == best, blk, nbp), axis=0, keepdims=True)
            hit = blk == idx
            bias = jnp.where(jnp.logical_and(hit, past), 0.0, bias)
            gate = jnp.where(hit, -jnp.inf, gate)
        bias = jnp.transpose(bias)
        if a == 0:
            bias = pltpu.roll(bias, HEAD_DIM, 1)
        q_aug.append(jnp.where(mask, q, bias.astype(BF16)))
        qa = jnp.where(mask, q, jnp.zeros_like(q))
        s = jnp.where(own_ok, _dot_nt(kaug_ref[a, own_rows, :], qa), -jnp.inf)
        m = jnp.max(s, axis=0, keepdims=True)
        p = jnp.exp(s - m)
        state += [m, jnp.sum(p, axis=0, keepdims=True), _dot(vt_own, p.astype(BF16))]

    n_chunks = (i + kc - 1) // kc
    last_chunk = k_ref.shape[0] // chunk - 1

    def chunk_rows(c):
        return pl.ds(pl.multiple_of(jnp.minimum(c, last_chunk) * chunk, chunk), chunk)

    def score(c, dst):
        rows = chunk_rows(c)
        for a in range(2):
            dst[a] = _dot_nt(kaug_ref[a, rows, :], q_aug[a])

    def absorb(c, src, st):
        vt = vt_ref[:, chunk_rows(c)]
        out = []
        for a in range(2):
            m, l, acc = st[3 * a], st[3 * a + 1], st[3 * a + 2]
            s = src[a]
            m_new = jnp.maximum(m, jnp.max(s, axis=0, keepdims=True))
            alpha = jnp.exp(m - m_new)
            p = jnp.exp(s - m_new)
            l = alpha * l + jnp.sum(p, axis=0, keepdims=True)
            acc = alpha * acc + _dot(vt, p.astype(BF16))
            out += [m_new, l, acc]
        return tuple(out)

    def body(t, st):
        c = 2 * t
        score(c + 1, s1_ref)
        st = absorb(c, s0_ref, st)
        score(c + 2, s0_ref)
        return absorb(c + 1, s1_ref, st)

    score(0, s0_ref)
    st = lax.fori_loop(0, (n_chunks + 1) // 2, body, tuple(state))
    sub = lax.broadcasted_iota(jnp.int32, (LANES, tq), 0)
    top = sub < HEAD_DIM
    o = jnp.where(top, st[2] / st[1], st[5] / st[4])
    sq = o * o
    ms_a = jnp.sum(jnp.where(top, sq, 0.0), axis=0, keepdims=True) * (1.0 / HEAD_DIM)
    ms_b = jnp.sum(jnp.where(top, 0.0, sq), axis=0, keepdims=True) * (1.0 / HEAD_DIM)
    r = jnp.where(top, lax.rsqrt(ms_a + RMS_EPS), lax.rsqrt(ms_b + RMS_EPS))
    o_ref[...] = jnp.transpose(o * r * g_ref[...]).astype(o_ref.dtype)


def _moba_prompt(qf, q16, k16, v16_t, kmean_p, g_col):
    t, w = q16.shape
    tq = MOBA_BLOCK
    nbp = kmean_p.shape[0]
    assert nbp == LANES and t // tq <= HEAD_DIM
    kc = MOBA_KEY_CHUNK
    while (t // tq) % kc:
        kc //= 2
    return pl.pallas_call(
        functools.partial(_moba_prompt_kernel, tq=tq, nbp=nbp, kc=kc),
        grid=(w // LANES, t // tq),
        in_specs=[pl.BlockSpec((tq, LANES), lambda p, i: (i, p)),
                  pl.BlockSpec((tq, LANES), lambda p, i: (i, p)),
                  pl.BlockSpec((t, LANES), lambda p, i: (0, p)),
                  pl.BlockSpec((LANES, t), lambda p, i: (p, 0)),
                  pl.BlockSpec((nbp, LANES), lambda p, i: (0, p)),
                  pl.BlockSpec((LANES, 1), lambda p, i: (p, 0))],
        out_specs=pl.BlockSpec((tq, LANES), lambda p, i: (i, p)),
        out_shape=jax.ShapeDtypeStruct((t, w), BF16),
        scratch_shapes=[pltpu.VMEM((2, t, LANES), BF16),
                        pltpu.VMEM((2, kc * tq, tq), F32), pltpu.VMEM((2, kc * tq, tq), F32)],
        compiler_params=_params("arbitrary", "arbitrary"),
        name="moba_prompt_attn",
    )(qf, q16, k16, v16_t, kmean_p, g_col)


def _outproj_kernel(osb_ref, omb_ref, wa_ref, wb_ref, x_ref, ga_ref, g_ref, sc_ref, sh_ref,
                    xo_ref, h_ref):
    mix = _dot(osb_ref[...], wa_ref[...]) + _dot(omb_ref[...], wb_ref[...])
    x = x_ref[...] + ga_ref[...] * mix
    xo_ref[...] = x
    h_ref[...] = (_rms(x) * g_ref[...] * (1.0 + sc_ref[...]) + sh_ref[...]).astype(h_ref.dtype)


def _outproj(o_sb, o_mb, w_a, w_b, x, ga, g, sc, sh, tm):
    t, d = x.shape
    w = o_sb.shape[1]
    row = lambda width: pl.BlockSpec((tm, width), lambda i: (i, 0))
    mod = lambda a: _row_spec(a.shape[0], tm, d)
    return pl.pallas_call(
        _outproj_kernel,
        grid=(t // tm,),
        in_specs=[row(w), row(w), _const_spec(w_a.shape), _const_spec(w_b.shape), row(d),
                  mod(ga), _const_spec((1, d)), mod(sc), mod(sh)],
        out_specs=[row(d), row(d)],
        out_shape=[jax.ShapeDtypeStruct((t, d), F32), jax.ShapeDtypeStruct((t, d), BF16)],
        compiler_params=_params("arbitrary"),
        name="out_proj",
    )(o_sb, o_mb, w_a, w_b, x, ga, g, sc, sh)


FFN_CHUNK = 256


def _silu_gate(val, gate):
    return val * gate * (1.0 / (1.0 + jnp.exp(-gate)))


def _ffn_prompt_kernel(h_ref, wup_ref, cw_ref, cb_ref, wdn_ref, x_ref, ga_ref, prev_ref,
                       xo_ref, state_ref, carry_ref, *, d_ff, tm):
    i = pl.program_id(0)

    @pl.when(i == 0)
    def _():
        carry_ref[...] = prev_ref[...]

    h = h_ref[...]
    row = lax.broadcasted_iota(jnp.int32, (tm, FFN_CHUNK), 0)

    def conv(off):
        cols = pl.ds(off, FFN_CHUNK)
        u = _dot(h, wup_ref[:, cols])
        prev = carry_ref[:, cols]
        p1, p2 = prev[SUBLANES - 1:SUBLANES, :], prev[SUBLANES - 2:SUBLANES - 1, :]
        u1 = jnp.where(row == 0, p1, pltpu.roll(u, 1, 0))
        u2 = jnp.where(row == 0, p2, jnp.where(row == 1, p1, pltpu.roll(u, 2, 0)))
        carry_ref[:, cols] = u[tm - SUBLANES:, :]
        cw = cw_ref[:, cols]
        return cb_ref[:, cols] + u2 * cw[0:1, :] + u1 * cw[1:2, :] + u * cw[2:3, :]

    acc = jnp.zeros((tm, xo_ref.shape[1]), F32)
    for c in range(d_ff // FFN_CHUNK):
        off = c * FFN_CHUNK
        act = _silu_gate(conv(off), conv(d_ff + off))
        acc = acc + _dot(act.astype(BF16), wdn_ref[pl.ds(off, FFN_CHUNK), :])
    xo_ref[...] = x_ref[...] + ga_ref[...] * acc

    @pl.when(i == pl.num_programs(0) - 1)
    def _():
        state_ref[...] = carry_ref[...]


def _ffn_prompt(h, w_up, conv_w, conv_b, w_down, x, ga, prev8, tm):
    t, d = x.shape
    d_ff = w_down.shape[0]
    row = lambda width: pl.BlockSpec((tm, width), lambda i: (i, 0))
    return pl.pallas_call(
        functools.partial(_ffn_prompt_kernel, d_ff=d_ff, tm=tm),
        grid=(t // tm,),
        in_specs=[row(d), _const_spec(w_up.shape), _const_spec(conv_w.shape),
                  _const_spec(conv_b.shape), _const_spec(w_down.shape), row(d),
                  _row_spec(ga.shape[0], tm, d), _const_spec(prev8.shape)],
        out_specs=[row(d), pl.BlockSpec(prev8.shape, lambda i: (0, 0))],
        out_shape=[jax.ShapeDtypeStruct((t, d), F32), jax.ShapeDtypeStruct(prev8.shape, F32)],
        scratch_shapes=[pltpu.VMEM(prev8.shape, F32)],
        compiler_params=_params("arbitrary"),
        name="conv_ffn_prompt",
    )(h, w_up, conv_w, conv_b, w_down, x, ga, prev8)


def _ffn_decode_kernel(h_ref, wup_ref, cw_ref, cb_ref, wdn_ref, x_ref, ga_ref, p0_ref, p1_ref,
                       xo_ref, u_ref, *, d_ff):
    h = h_ref[...]

    def conv(off):
        cols = pl.ds(off, FFN_CHUNK)
        u = _dot(h, wup_ref[:, cols])
        u_ref[:, cols] = u
        cw = cw_ref[:, cols]
        return (cb_ref[:, cols] + p0_ref[:, cols] * cw[0:1, :] + p1_ref[:, cols] * cw[1:2, :]
                + u * cw[2:3, :])

    acc = jnp.zeros(xo_ref.shape, F32)
    for c in range(d_ff // FFN_CHUNK):
        off = c * FFN_CHUNK
        act = _silu_gate(conv(off), conv(d_ff + off))
        acc = acc + _dot(act.astype(BF16), wdn_ref[pl.ds(off, FFN_CHUNK), :])
    xo_ref[...] = x_ref[...] + ga_ref[...] * acc


def _ffn_decode(h, w_up, conv_w, conv_b, w_down, x, ga, prev0, prev1):
    b, d = x.shape
    d_ff = w_down.shape[0]
    full = lambda a: pl.BlockSpec(a.shape, lambda i: (0,) * a.ndim)
    args = (h, w_up, conv_w, conv_b, w_down, x, ga, prev0, prev1)
    return pl.pallas_call(
        functools.partial(_ffn_decode_kernel, d_ff=d_ff),
        grid=(1,),
        in_specs=[full(a) for a in args],
        out_specs=[pl.BlockSpec((b, d), lambda i: (0, 0)),
                   pl.BlockSpec((b, 2 * d_ff), lambda i: (0, 0))],
        out_shape=[jax.ShapeDtypeStruct((b, d), F32), jax.ShapeDtypeStruct((b, 2 * d_ff), F32)],
        compiler_params=_params("arbitrary"),
        name="conv_ffn_decode",
    )(*args)


def _final_norm_kernel(x_ref, g_ref, o_ref):
    o_ref[...] = _rms(x_ref[...]) * g_ref[...]


def _final_norm(x, g, tm):
    t, d = x.shape
    return pl.pallas_call(
        _final_norm_kernel,
        grid=(t // tm,),
        in_specs=[pl.BlockSpec((tm, d), lambda i: (i, 0)), pl.BlockSpec((1, d), lambda i: (0, 0))],
        out_specs=pl.BlockSpec((tm, d), lambda i: (i, 0)),
        out_shape=jax.ShapeDtypeStruct((t, d), F32),
        compiler_params=_params("arbitrary"),
        name="final_norm",
    )(x, g)


def _page_scores(qb, k_page, heads):
    sub = lax.broadcasted_iota(jnp.int32, (heads, k_page.shape[-1]), 0)
    z = jnp.zeros(sub.shape, F32)
    for h in range(heads):
        row = jnp.sum(qb[h] * k_page[h], axis=0, keepdims=True)
        z = jnp.where(sub == h, row, z)
    return z


def _sb_decode_kernel(pt_ref, qb_ref, k_hbm, v_hbm, g_ref, o_ref, kbuf, vbuf, acc_ref, sem,
                      *, layer, n_pages, group, heads):
    b = pl.program_id(0)
    n_chunks = n_pages // group
    page = kbuf.shape[-1]
    lane = lax.broadcasted_iota(jnp.int32, (heads, page), 1)
    qb = qb_ref.at[0]

    def copies(c, slot):
        out = []
        for g in range(group):
            pg = pt_ref[b, c * group + g]
            out.append(pltpu.make_async_copy(k_hbm.at[layer, pg], kbuf.at[slot, g], sem.at[0, slot]))
            out.append(pltpu.make_async_copy(v_hbm.at[layer, pg], vbuf.at[slot, g], sem.at[1, slot]))
        return out

    def start(c, slot):
        for cp in copies(c, slot):
            cp.start()

    def wait(c, slot):
        for cp in copies(c, slot):
            cp.wait()

    def page_update(slot, g, keep):
        z = _page_scores(qb, kbuf.at[slot, g], heads) * HEAD_DIM ** -0.5
        sp = _softplus(z)
        suffix = sp
        shift = 1
        while shift < page:
            moved = pltpu.roll(suffix, page - shift, 1)
            suffix = suffix + jnp.where(lane + shift < page, moved, 0.0)
            shift *= 2
        wgt = jnp.exp(z - sp - (suffix - sp) - keep)
        for h in range(heads):
            acc_ref[h] += wgt[h:h + 1, :] * vbuf[slot, g, h]
        return keep + jnp.sum(sp, axis=1, keepdims=True)

    def step(carry):
        c, _, keep = carry
        slot = c % 2

        @pl.when(c > 0)
        def _():
            start(c - 1, 1 - slot)

        wait(c, slot)
        for g in reversed(range(group)):
            keep = page_update(slot, g, keep)
        return c - 1, jnp.min(keep) < SB_EXIT, keep

    acc_ref[...] = jnp.zeros_like(acc_ref)
    last = n_chunks - 1
    start(last, last % 2)
    c_end, _, _ = lax.while_loop(lambda cr: jnp.logical_and(cr[0] >= 0, cr[1]), step,
                                 (jnp.int32(last), jnp.bool_(True), jnp.zeros((heads, 1), F32)))

    @pl.when(c_end >= 0)
    def _():
        wait(c_end, c_end % 2)

    for h in range(heads):
        o = jnp.sum(acc_ref[h], axis=1, keepdims=True)
        o_ref[0, h] = o * lax.rsqrt(jnp.mean(o * o, axis=0, keepdims=True) + RMS_EPS) * g_ref[h]


def _sb_decode(page_table, qb, cache_k, cache_v, layer, g):
    b, heads, hd, page = qb.shape
    n_pages = page_table.shape[1]
    group = DECODE_PAGE_GROUP
    while n_pages % group:
        group //= 2
    grid_spec = pltpu.PrefetchScalarGridSpec(
        num_scalar_prefetch=1,
        grid=(b,),
        in_specs=[pl.BlockSpec((1, heads, hd, page), lambda i, pt: (i, 0, 0, 0)),
                  pl.BlockSpec(memory_space=pl.ANY), pl.BlockSpec(memory_space=pl.ANY),
                  pl.BlockSpec((heads, hd, 1), lambda i, pt: (0, 0, 0))],
        out_specs=pl.BlockSpec((1, heads, hd, 1), lambda i, pt: (i, 0, 0, 0)),
        scratch_shapes=[pltpu.VMEM((2, group, heads, hd, page), F32),
                        pltpu.VMEM((2, group, heads, hd, page), F32),
                        pltpu.VMEM((heads, hd, page), F32),
                        pltpu.SemaphoreType.DMA((2, 2))],
    )
    return pl.pallas_call(
        functools.partial(_sb_decode_kernel, layer=layer, n_pages=n_pages, group=group, heads=heads),
        grid_spec=grid_spec,
        out_shape=jax.ShapeDtypeStruct((b, heads, hd, 1), F32),
        compiler_params=_params("arbitrary"),
        name="sb_decode_attn",
    )(page_table, qb, cache_k, cache_v, g)


def _moba_gate_kernel(pt_ref, qb_ref, *refs, group, ppb, heads):
    del pt_ref
    k_refs = refs[:group]
    sel_ref, gate_ref = refs[group], refs[group + 1]
    s = pl.program_id(1)
    page = k_refs[0].shape[-1]
    lane = lax.broadcasted_iota(jnp.int32, (heads, page), 1)
    qb = qb_ref.at[0]

    @pl.when(s == 0)
    def _():
        gate_ref[...] = jnp.full(gate_ref.shape, -jnp.inf, F32)

    gates = gate_ref[...]
    for blk in range(group // ppb):
        z = _page_scores(qb, k_refs[blk * ppb], heads)
        for o in range(1, ppb):
            z = z + _page_scores(qb, k_refs[blk * ppb + o], heads)
        gate = jnp.sum(z, axis=1, keepdims=True) * (1.0 / MOBA_BLOCK)
        gates = jnp.where(lane == s * (group // ppb) + blk, gate, gates)
    gate_ref[...] = gates

    @pl.when(s == pl.num_programs(1) - 1)
    def _():
        gates = gate_ref[...]
        for r in range(MOBA_TOPK):
            best = jnp.max(gates, axis=1, keepdims=True)
            idx = jnp.min(jnp.where(gates == best, lane, page), axis=1, keepdims=True)
            sel_ref[0, r] = jnp.broadcast_to(idx, (heads, page))
            gates = jnp.where(lane == idx, -jnp.inf, gates)


def _moba_gate(page_table, qb, cache_k, layer):
    b, heads, hd, page = qb.shape
    n_pages = page_table.shape[1]
    ppb = MOBA_BLOCK // page
    assert n_pages // ppb <= page
    group = max(ppb, GATE_PAGE_GROUP)
    while n_pages % group:
        group -= ppb
    cache_specs = [pl.BlockSpec((None, None, heads, hd, page),
                                functools.partial(lambda i, s, pt, o: (layer, pt[i, group * s + o], 0, 0, 0), o=o))
                   for o in range(group)]
    grid_spec = pltpu.PrefetchScalarGridSpec(
        num_scalar_prefetch=1,
        grid=(b, n_pages // group),
        in_specs=[pl.BlockSpec((1, heads, hd, page), lambda i, s, pt: (i, 0, 0, 0))] + cache_specs,
        out_specs=pl.BlockSpec((1, MOBA_TOPK, heads, page), lambda i, s, pt: (i, 0, 0, 0)),
        scratch_shapes=[pltpu.VMEM((heads, page), F32)],
    )
    sel = pl.pallas_call(
        functools.partial(_moba_gate_kernel, group=group, ppb=ppb, heads=heads),
        grid_spec=grid_spec,
        out_shape=jax.ShapeDtypeStruct((b, MOBA_TOPK, heads, page), jnp.int32),
        compiler_params=_params("arbitrary", "arbitrary"),
        name="moba_decode_gate",
    )(page_table, qb, *([cache_k] * group))
    return sel[..., 0]


def _moba_decode_kernel(pt_ref, sel_ref, qb_ref, kn_ref, vn_ref, *refs, n_sel, hps):
    del pt_ref, sel_ref
    n_pages = n_sel * hps
    k_refs, v_refs = refs[:n_pages], refs[n_pages:2 * n_pages]
    g_ref, o_ref = refs[2 * n_pages], refs[2 * n_pages + 1]
    scale = HEAD_DIM ** -0.5
    for hh in range(hps):
        qb = qb_ref[hh]
        ks, vs = k_refs[hh * n_sel:(hh + 1) * n_sel], v_refs[hh * n_sel:(hh + 1) * n_sel]
        scores = [jnp.sum(qb * r[...], axis=0, keepdims=True) * scale for r in ks]
        s_own = jnp.sum(qb[:, :1] * kn_ref[hh], axis=0, keepdims=True) * scale
        m = s_own
        for s in scores:
            m = jnp.maximum(m, jnp.max(s, axis=1, keepdims=True))
        p_own = jnp.exp(s_own - m)
        l = p_own
        acc = jnp.zeros(qb.shape, F32)
        for s, v_ref in zip(scores, vs):
            p = jnp.exp(s - m)
            l = l + jnp.sum(p, axis=1, keepdims=True)
            acc = acc + p * v_ref[...]
        o = (jnp.sum(acc, axis=1, keepdims=True) + p_own * vn_ref[hh]) / l
        o_ref[hh] = o * lax.rsqrt(jnp.mean(o * o, axis=0, keepdims=True) + RMS_EPS) * g_ref[hh]


def _moba_decode(page_table, sel_flat, qb, k_new, v_new, cache_k, cache_v, layer, g):
    b, heads, hd, page = qb.shape
    ppb = MOBA_BLOCK // page
    n_sel = MOBA_TOPK * ppb
    hps = MOBA_DECODE_HEADS
    while heads % hps:
        hps //= 2

    def cache_map(i, hp, pt, sel, hh, n):
        h = hp * hps + hh
        blk = sel[(i * MOBA_TOPK + n // ppb) * heads + h]
        return (layer, pt[i, blk * ppb + n % ppb], h, 0, 0)

    cache_specs = [pl.BlockSpec((None, None, None, hd, page), functools.partial(cache_map, hh=hh, n=n))
                   for hh in range(hps) for n in range(n_sel)]
    head_spec = lambda last: pl.BlockSpec((None, hps, hd, last), lambda i, hp, pt, sel: (i, hp, 0, 0))
    grid_spec = pltpu.PrefetchScalarGridSpec(
        num_scalar_prefetch=2,
        grid=(b, heads // hps),
        in_specs=[head_spec(page), head_spec(1), head_spec(1)] + cache_specs + cache_specs
                 + [pl.BlockSpec((hps, hd, 1), lambda i, hp, pt, sel: (hp, 0, 0))],
        out_specs=head_spec(1),
    )
    n_pages = n_sel * hps
    return pl.pallas_call(
        functools.partial(_moba_decode_kernel, n_sel=n_sel, hps=hps),
        grid_spec=grid_spec,
        out_shape=jax.ShapeDtypeStruct((b, heads, hd, 1), F32),
        compiler_params=_params("arbitrary", "arbitrary"),
        name="moba_decode_attn",
    )(page_table, sel_flat, qb, k_new, v_new, *([cache_k] * n_pages), *([cache_v] * n_pages), g)


def _rope_tables(pos):
    half = HEAD_DIM // 2
    inv_freq = ROPE_THETA ** (-jnp.arange(half, dtype=F32) / half)
    ang = pos.astype(F32)[:, None] * inv_freq[None, :]
    cos, sin = jnp.cos(ang), jnp.sin(ang)
    zero = jnp.zeros_like(sin)
    reps = LANES // HEAD_DIM
    cos_t = jnp.tile(jnp.concatenate([cos, cos], axis=1), (1, reps))
    sin_a = jnp.tile(jnp.concatenate([-sin, zero], axis=1), (1, reps))
    sin_b = jnp.tile(jnp.concatenate([zero, sin], axis=1), (1, reps))
    return cos_t, sin_a, sin_b


def _row_tile(t, target):
    tm = min(t, target)
    while t % tm:
        tm //= 2
    return tm


def kernel(x_prompt, x_sample, c_prompt, c_sample, cache_sb_k, cache_sb_v, cache_moba_k, cache_moba_v, state_ffn_conv, page_table, w_ada, b_ada, g_attn, w_in, g_sb_out, g_moba_out, w_out, g_ffn, w_up, conv_w, conv_b, w_down, g_final):
    batch, seq, d = x_prompt.shape
    dec_b, dec_seq, _ = x_sample.shape
    depth = w_ada.shape[0]
    n_pages, page = page_table.shape[1], cache_sb_k.shape[2]
    heads = cache_sb_k.shape[3]
    w = heads * HEAD_DIM
    d_ff = w_down.shape[1]
    past_len = n_pages * page
    assert dec_seq == 1 and d == 2 * w and w_in.shape[2] == 6 * w
    assert seq % MOBA_BLOCK == 0 and past_len % MOBA_BLOCK == 0 and MOBA_BLOCK % page == 0
    assert past_len // MOBA_BLOCK >= MOBA_TOPK and d_ff % FFN_CHUNK == 0

    n_c = batch + dec_b
    c_all = jnp.concatenate([c_prompt, c_sample], axis=0)
    c_all = jnp.pad(c_all, ((0, -n_c % SUBLANES), (0, 0)))
    mod = _ada(c_all, w_ada, b_ada)

    sb_k_t, sb_v_t, mb_k_t, mb_v_t = (jnp.transpose(c, (0, 1, 3, 4, 2)) for c in
                                      (cache_sb_k, cache_sb_v, cache_moba_k, cache_moba_v))
    rope_p = _rope_tables(jnp.arange(seq))
    rope_s = _rope_tables(jnp.full((dec_b,), past_len))
    nb = seq // MOBA_BLOCK
    nbp = -(-nb // LANES) * LANES
    tm_p = _row_tile(seq, 512)
    xs = x_sample.reshape(dec_b, d)
    xps = [x_prompt[b] for b in range(batch)]
    zeros_prev = jnp.zeros((SUBLANES, 2 * d_ff), F32)

    rows_p = [[] for _ in range(4)]
    rows_s = [[] for _ in range(4)]
    conv_p, conv_s = [], []
    for l in range(depth):
        wl = w_in[l]
        wqk = jnp.concatenate([wl[:, 3 * w:4 * w], wl[:, 4 * w:5 * w]], axis=1)
        wqk_hi = wqk.astype(BF16)
        wqk_lo = (wqk - wqk_hi.astype(F32)).astype(BF16)
        wts = (wl[:, :3 * w].astype(BF16), wqk_hi, wqk_lo, wl[:, 5 * w:].astype(BF16))
        wo_a, wo_b = w_out[l, :w].astype(BF16), w_out[l, w:].astype(BF16)
        wup, wdn = w_up[l].astype(BF16), w_down[l].astype(BF16)
        cw, cb = conv_w[l], conv_b[l].reshape(1, -1)
        g_a, g_f = g_attn[l].reshape(1, d), g_ffn[l].reshape(1, d)
        g_sb, g_mb = g_sb_out[l].reshape(1, w), g_moba_out[l].reshape(1, w)

        def mods(lo, hi):
            m = mod[l, lo:hi]
            return [m[:, k * d:(k + 1) * d] for k in range(6)]

        layer_rows = [[] for _ in range(4)]
        layer_conv = []
        for b in range(batch):
            sh1, sc1, ga1, sh2, sc2, ga2 = mods(b, b + 1)
            (k_sb, v_sb, k_mb, v_mb, q_mbf, q_sb16, k_sb16, v_sb16, q_mb16, k_mb16,
             v_mb16) = _proj(xps[b], sc1, sh1, g_a, wts, rope_p, tm_p)
            o_sb = _sb_prompt(q_sb16, k_sb16, v_sb16, g_sb, MOBA_BLOCK)
            kmean = jnp.pad(_kmean(k_mb), ((0, nbp - nb), (0, 0)))
            o_mb = _moba_prompt(q_mbf, q_mb16, k_mb16, v_mb16.T, kmean, g_mb.reshape(w, 1))
            x_mid, h2 = _outproj(o_sb, o_mb, wo_a, wo_b, xps[b], ga1, g_f, sc2, sh2, tm_p)
            xps[b], state8 = _ffn_prompt(h2, wup, cw, cb, wdn, x_mid, ga2, zeros_prev, tm_p)
            for lst, a in zip(layer_rows, (k_sb, v_sb, k_mb, v_mb)):
                lst.append(a.reshape(seq, heads, HEAD_DIM))
            layer_conv.append(state8[SUBLANES - (CONV_W - 1):])
        for lst, parts in zip(rows_p, layer_rows):
            lst.append(jnp.stack(parts))
        conv_p.append(jnp.stack(layer_conv))

        sh1, sc1, ga1, sh2, sc2, ga2 = mods(batch, n_c)
        (k_sb, v_sb, k_mb, v_mb, q_mbf, q_sb16, _, _, _, _, _) = _proj(
            xs, sc1, sh1, g_a, wts, rope_s, dec_b)
        col = lambda a: a.reshape(dec_b, heads, HEAD_DIM, 1)
        lanes = lambda a: jnp.broadcast_to(col(a), (dec_b, heads, HEAD_DIM, page))
        q_sb = q_sb16.astype(F32) * HEAD_DIM ** 0.5
        o_sb = _sb_decode(page_table, lanes(q_sb), sb_k_t, sb_v_t, l, g_sb.reshape(heads, HEAD_DIM, 1))
        qb_mb = lanes(q_mbf)
        sel = _moba_gate(page_table, qb_mb, mb_k_t, l)
        o_mb = _moba_decode(page_table, sel.reshape(-1), qb_mb, col(k_mb), col(v_mb),
                            mb_k_t, mb_v_t, l, g_mb.reshape(heads, HEAD_DIM, 1))
        x_mid, h2 = _outproj(o_sb.reshape(dec_b, w).astype(BF16), o_mb.reshape(dec_b, w).astype(BF16),
                             wo_a, wo_b, xs, ga1, g_f, sc2, sh2, dec_b)
        prev = state_ffn_conv[l]
        xs, u_new = _ffn_decode(h2, wup, cw, cb, wdn, x_mid, ga2, prev[:, 0], prev[:, 1])
        for lst, a in zip(rows_s, (k_sb, v_sb, k_mb, v_mb)):
            lst.append(a.reshape(dec_b, 1, heads, HEAD_DIM))
        conv_s.append(jnp.stack([prev[:, 1], u_new], axis=1))

    g_fin = g_final.reshape(1, d)
    y_prompt = jnp.stack([_final_norm(xp, g_fin, tm_p) for xp in xps])
    y_sample = _final_norm(xs, g_fin, dec_b).reshape(dec_b, 1, d)
    return (y_prompt, y_sample,
            jnp.stack(rows_p[0]), jnp.stack(rows_p[1]), jnp.stack(rows_p[2]), jnp.stack(rows_p[3]),
            jnp.stack(conv_p),
            jnp.stack(rows_s[0]), jnp.stack(rows_s[1]), jnp.stack(rows_s[2]), jnp.stack(rows_s[3]),
            jnp.stack(conv_s))
```

```python
import functools

import jax
import jax.numpy as jnp
from jax import lax
from jax.experimental import pallas as pl
from jax.experimental.pallas import tpu as pltpu

HEAD_DIM = 64
MOBA_BLOCK = 256
MOBA_TOPK = 3
ROPE_THETA = 10000.0
CONV_W = 3
RMS_EPS = 1e-6

LANES = 128
SUBLANES = 8
VMEM_LIMIT_BYTES = 56 * 1024 * 1024
NEG_BIG = -1e30
SB_EXIT = 110.0
MOBA_KEY_CHUNK = 4
DECODE_PAGE_GROUP = 4
GATE_PAGE_GROUP = 16
MOBA_DECODE_HEADS = 2

F32 = jnp.float32
BF16 = jnp.bfloat16
NT_DIMS = (((1,), (1,)), ((), ()))


def _params(*sem, flags=None):
    return pltpu.CompilerParams(dimension_semantics=sem, vmem_limit_bytes=VMEM_LIMIT_BYTES,
                                flags=flags)


def _dot(a, b):
    return jnp.dot(a, b, preferred_element_type=F32)


def _dot_nt(a, b):
    return lax.dot_general(a, b, NT_DIMS, preferred_element_type=F32)


def _split(x):
    hi = x.astype(BF16)
    lo = (x - hi.astype(F32)).astype(BF16)
    return hi, lo


def _rms(x):
    return x * lax.rsqrt(jnp.mean(x * x, axis=-1, keepdims=True) + RMS_EPS)


def _softplus(z):
    return jnp.maximum(z, 0.0) + jnp.log(1.0 + jnp.exp(-jnp.abs(z)))


def _row_spec(rows, tm, width):
    if rows == 1:
        return pl.BlockSpec((1, width), lambda i: (0, 0))
    return pl.BlockSpec((tm, width), lambda i: (i, 0))


def _const_spec(shape):
    return pl.BlockSpec(shape, lambda *_: (0,) * len(shape), pipeline_mode=pl.Buffered(1))


def _ada_kernel(c_ref, w_ref, b_ref, o_ref):
    c = c_ref[...]
    s = c * (1.0 / (1.0 + jnp.exp(-c)))
    o_ref[0] = jnp.dot(s, w_ref[0], preferred_element_type=F32,
                       precision=lax.Precision.HIGHEST) + b_ref[0]


def _ada(c_all, w_ada, b_ada):
    depth, d, n = w_ada.shape
    r = c_all.shape[0]
    tn = 1536 if n % 1536 == 0 else n
    return pl.pallas_call(
        _ada_kernel,
        grid=(depth, n // tn),
        in_specs=[pl.BlockSpec((r, d), lambda l, j: (0, 0)),
                  pl.BlockSpec((1, d, tn), lambda l, j: (l, 0, j)),
                  pl.BlockSpec((1, 1, tn), lambda l, j: (l, 0, j))],
        out_specs=pl.BlockSpec((1, r, tn), lambda l, j: (l, 0, j)),
        out_shape=jax.ShapeDtypeStruct((depth, r, n), F32),
        compiler_params=_params("arbitrary", "arbitrary"),
        name="ada_mod",
    )(c_all, w_ada, b_ada.reshape(depth, 1, n))


def _rope(x, cos, sin_a, sin_b):
    outs = []
    for c in range(x.shape[1] // LANES):
        xc = x[:, c * LANES:(c + 1) * LANES]
        outs.append(xc * cos + pltpu.roll(xc, LANES - HEAD_DIM // 2, 1) * sin_a
                    + pltpu.roll(xc, HEAD_DIM // 2, 1) * sin_b)
    return jnp.concatenate(outs, axis=1)


def _proj_kernel(x_ref, sc_ref, sh_ref, g_ref, wsb_ref, wqk_hi_ref, wqk_lo_ref, wmv_ref,
                 cos_ref, sa_ref, sb_ref,
                 ksb_ref, vsb_ref, kmb_ref, vmb_ref, qmbf_ref,
                 qsb16_ref, ksb16_ref, vsb16_ref, qmb16_ref, kmb16_ref, vmb16_ref, *, w):
    h = _rms(x_ref[...]) * g_ref[...] * (1.0 + sc_ref[...]) + sh_ref[...]
    h_hi, h_lo = _split(h)
    scale = HEAD_DIM ** -0.5

    sb = _dot(h_hi, wsb_ref[...])
    q_sb, k_sb, v_sb = sb[:, :w], sb[:, w:2 * w], sb[:, 2 * w:]
    ksb_ref[...] = k_sb
    vsb_ref[...] = v_sb
    qsb16_ref[...] = (q_sb * scale).astype(BF16)
    ksb16_ref[...] = k_sb.astype(BF16)
    vsb16_ref[...] = v_sb.astype(BF16)

    w_hi = wqk_hi_ref[...]
    qk = _dot(h_hi, w_hi) + _dot(h_hi, wqk_lo_ref[...]) + _dot(h_lo, w_hi)
    cos, sa, sb_t = cos_ref[...], sa_ref[...], sb_ref[...]
    q_mb = _rope(qk[:, :w], cos, sa, sb_t)
    k_mb = _rope(qk[:, w:], cos, sa, sb_t)
    v_mb = _dot(h_hi, wmv_ref[...])
    kmb_ref[...] = k_mb
    vmb_ref[...] = v_mb
    qmbf_ref[...] = q_mb
    qmb16_ref[...] = (q_mb * scale).astype(BF16)
    kmb16_ref[...] = k_mb.astype(BF16)
    vmb16_ref[...] = v_mb.astype(BF16)


def _proj(x, sc, sh, g, wts, rope_tabs, tm):
    t, d = x.shape
    wsb, wqk_hi, wqk_lo, wmv = wts
    w = wmv.shape[1]
    cos, sa, sb = rope_tabs
    row = lambda width: pl.BlockSpec((tm, width), lambda i: (i, 0))
    f32_out = jax.ShapeDtypeStruct((t, w), F32)
    b16_out = jax.ShapeDtypeStruct((t, w), BF16)
    return pl.pallas_call(
        functools.partial(_proj_kernel, w=w),
        grid=(t // tm,),
        in_specs=[row(d), _row_spec(sc.shape[0], tm, d), _row_spec(sh.shape[0], tm, d),
                  _const_spec((1, d)), _const_spec(wsb.shape), _const_spec(wqk_hi.shape),
                  _const_spec(wqk_lo.shape), _const_spec(wmv.shape),
                  row(LANES), row(LANES), row(LANES)],
        out_specs=[row(w)] * 11,
        out_shape=[f32_out] * 5 + [b16_out] * 6,
        compiler_params=_params("arbitrary"),
        name="qkv_proj",
    )(x, sc, sh, g, wsb, wqk_hi, wqk_lo, wmv, cos, sa, sb)


def _head_norm_pair(o, first, g):
    sq = o * o
    ms_a = jnp.sum(jnp.where(first, sq, 0.0), axis=1, keepdims=True) * (1.0 / HEAD_DIM)
    ms_b = jnp.sum(jnp.where(first, 0.0, sq), axis=1, keepdims=True) * (1.0 / HEAD_DIM)
    r = jnp.where(first, lax.rsqrt(ms_a + RMS_EPS), lax.rsqrt(ms_b + RMS_EPS))
    return o * r * g


def _sb_prompt_kernel(q_ref, k_ref, v_ref, g_ref, o_ref, *, tq):
    i = pl.program_id(1)
    q = q_ref[...]
    lane = lax.broadcasted_iota(jnp.int32, (tq, LANES), 1)
    first = lane < HEAD_DIM
    zero = jnp.zeros_like(q)
    qs = (jnp.where(first, q, zero), jnp.where(first, zero, q))
    r_i = lax.broadcasted_iota(jnp.int32, (tq, tq), 0)
    c_i = lax.broadcasted_iota(jnp.int32, (tq, tq), 1)
    neg_upper = jnp.where(r_i > c_i, -1.0, 0.0).astype(BF16)
    causal = c_i < r_i

    def tile(j, state, diag):
        start = pl.multiple_of(j * tq, tq)
        kt = k_ref[pl.ds(start, tq), :]
        vt = v_ref[pl.ds(start, tq), :]
        out = []
        for a in range(2):
            keep_sum, acc = state[2 * a], state[2 * a + 1]
            z = _dot_nt(qs[a], kt)
            sp = _softplus(z)
            spm = jnp.where(causal, sp, 0.0) if diag else sp
            hi, lo = _split(spm)
            between = _dot(hi, neg_upper) + _dot(lo, neg_upper)
            wgt = jnp.exp(z - sp + between - keep_sum)
            if diag:
                wgt = jnp.where(causal, wgt, 0.0)
            acc = acc + _dot(wgt.astype(BF16), vt)
            keep_sum = keep_sum + jnp.sum(spm, axis=1, keepdims=True)
            out += [keep_sum, acc]
        return tuple(out)

    def alive(st):
        return jnp.min(jnp.minimum(st[0], st[2])) < SB_EXIT

    def step(carry):
        st = tile(i - 1 - carry[0], carry[2:], False)
        return (carry[0] + 1, alive(st)) + st

    zc = jnp.zeros((tq, 1), F32)
    za = jnp.zeros((tq, LANES), F32)
    state = tile(i, (zc, za, zc, za), True)
    carry = lax.while_loop(lambda c: jnp.logical_and(c[0] < i, c[1]), step,
                           (jnp.int32(0), alive(state)) + state)
    state = carry[2:]
    o = jnp.where(first, state[1], state[3])
    o_ref[...] = _head_norm_pair(o, first, g_ref[...]).astype(o_ref.dtype)


def _sb_prompt(q16, k16, v16, g, tq):
    t, w = q16.shape
    return pl.pallas_call(
        functools.partial(_sb_prompt_kernel, tq=tq),
        grid=(w // LANES, t // tq),
        in_specs=[pl.BlockSpec((tq, LANES), lambda p, i: (i, p)),
                  pl.BlockSpec((t, LANES), lambda p, i: (0, p)),
                  pl.BlockSpec((t, LANES), lambda p, i: (0, p)),
                  pl.BlockSpec((1, LANES), lambda p, i: (0, p))],
        out_specs=pl.BlockSpec((tq, LANES), lambda p, i: (i, p)),
        out_shape=jax.ShapeDtypeStruct((t, w), BF16),
        compiler_params=_params("arbitrary", "arbitrary"),
        name="sb_prompt_attn",
    )(q16, k16, v16, g)


def _kmean_kernel(k_ref, o_ref):
    o_ref[0] = jnp.mean(k_ref[...], axis=0, keepdims=True)


def _kmean(k_mb):
    t, w = k_mb.shape
    nb = t // MOBA_BLOCK
    return pl.pallas_call(
        _kmean_kernel,
        grid=(nb,),
        in_specs=[pl.BlockSpec((MOBA_BLOCK, w), lambda n: (n, 0))],
        out_specs=pl.BlockSpec((1, 1, w), lambda n: (n, 0, 0)),
        out_shape=jax.ShapeDtypeStruct((nb, 1, w), F32),
        compiler_params=_params("arbitrary"),
        name="moba_kmean",
    )(k_mb).reshape(nb, w)


def _moba_prompt_kernel(qf_ref, q_ref, k_ref, vt_ref, km_ref, g_ref, o_ref, kaug_ref, s0_ref, s1_ref,
                        *, tq, nbp, kc):
    i = pl.program_id(1)
    q = q_ref[...]
    qf = qf_ref[...]
    km_hi, km_lo = _split(km_ref[...])
    lane = lax.broadcasted_iota(jnp.int32, (tq, LANES), 1)
    first = lane < HEAD_DIM
    blk = lax.broadcasted_iota(jnp.int32, (nbp, tq), 0)
    past = blk < i
    chunk = kc * tq

    @pl.when(i == 0)
    def _():
        k_lane = lax.broadcasted_iota(jnp.int32, (chunk, LANES), 1)
        k_blk = lax.broadcasted_iota(jnp.int32, (chunk, LANES), 0) // tq
        k_first = k_lane < HEAD_DIM

        def fill(c, carry):
            rows = pl.ds(pl.multiple_of(c * chunk, chunk), chunk)
            kt = k_ref[rows, :]
            key_blk = k_blk + c * kc
            hot_a = jnp.where(k_lane - HEAD_DIM == key_blk, 1.0, 0.0).astype(BF16)
            hot_b = jnp.where(k_lane == key_blk, 1.0, 0.0).astype(BF16)
            kaug_ref[0, rows, :] = jnp.where(k_first, kt, hot_a)
            kaug_ref[1, rows, :] = jnp.where(k_first, hot_b, kt)
            return carry

        lax.fori_loop(0, k_ref.shape[0] // chunk, fill, 0)

    key_i = lax.broadcasted_iota(jnp.int32, (tq, tq), 0)
    qry_i = lax.broadcasted_iota(jnp.int32, (tq, tq), 1)
    own_ok = key_i <= qry_i
    own_rows = pl.ds(pl.multiple_of(i * tq, tq), tq)
    head_rows = (slice(0, HEAD_DIM), slice(HEAD_DIM, LANES))

    q_aug, state = [], []
    for a in range(2):
        mask = first if a == 0 else jnp.logical_not(first)
        qf_hi, qf_lo = _split(jnp.where(mask, qf, 0.0))
        gate = _dot_nt(km_hi, qf_hi) + _dot_nt(km_lo, qf_hi) + _dot_nt(km_hi, qf_lo)
        gate = jnp.where(past, gate, -jnp.inf)
        bias = jnp.full((nbp, tq), NEG_BIG, F32)
        for _ in range(MOBA_TOPK):
            best = jnp.max(gate, axis=0, keepdims=True)
            idx = jnp.min(jnp.where(gate == best, blk, nbp), axis=0, keepdims=True)
            hit = blk == idx
            bias = jnp.where(jnp.logical_and(hit, past), 0.0, bias)
            gate = jnp.where(hit, -jnp.inf, gate)
        bias = jnp.transpose(bias)
        if a == 0:
            bias = pltpu.roll(bias, HEAD_DIM, 1)
        q_aug.append(jnp.where(mask, q, bias.astype(BF16)))
        qa = jnp.where(mask, q, jnp.zeros_like(q))
        s = jnp.where(own_ok, _dot_nt(kaug_ref[a, own_rows, :], qa), -jnp.inf)
        m = jnp.max(s, axis=0, keepdims=True)
        p = jnp.exp(s - m)
        state += [m, jnp.sum(p, axis=0, keepdims=True),
                  _dot(vt_ref[head_rows[a], own_rows], p.astype(BF16))]

    n_chunks = (i + kc - 1) // kc
    last_chunk = k_ref.shape[0] // chunk - 1

    def chunk_rows(c):
        return pl.ds(pl.multiple_of(jnp.minimum(c, last_chunk) * chunk, chunk), chunk)

    def score(c, dst):
        rows = chunk_rows(c)
        for a in range(2):
            dst[a] = _dot_nt(kaug_ref[a, rows, :], q_aug[a])

    def absorb(c, src, st):
        rows = chunk_rows(c)
        out = []
        for a in range(2):
            m, l, acc = st[3 * a], st[3 * a + 1], st[3 * a + 2]
            s = src[a]
            m_new = jnp.maximum(m, jnp.max(s, axis=0, keepdims=True))
            alpha = jnp.exp(m - m_new)
            p = jnp.exp(s - m_new)
            l = alpha * l + jnp.sum(p, axis=0, keepdims=True)
            acc = alpha * acc + _dot(vt_ref[head_rows[a], rows], p.astype(BF16))
            out += [m_new, l, acc]
        return tuple(out)

    def body(t, st):
        c = 2 * t
        score(c + 1, s1_ref)
        st = absorb(c, s0_ref, st)
        score(c + 2, s0_ref)
        return absorb(c + 1, s1_ref, st)

    score(0, s0_ref)
    st = lax.fori_loop(0, (n_chunks + 1) // 2, body, tuple(state))
    sub = lax.broadcasted_iota(jnp.int32, (LANES, tq), 0)
    top = sub < HEAD_DIM
    o = jnp.concatenate([st[2] / st[1], st[5] / st[4]], axis=0)
    sq = o * o
    ms_a = jnp.sum(jnp.where(top, sq, 0.0), axis=0, keepdims=True) * (1.0 / HEAD_DIM)
    ms_b = jnp.sum(jnp.where(top, 0.0, sq), axis=0, keepdims=True) * (1.0 / HEAD_DIM)
    r = jnp.where(top, lax.rsqrt(ms_a + RMS_EPS), lax.rsqrt(ms_b + RMS_EPS))
    o_ref[...] = jnp.transpose(o * r * g_ref[...]).astype(o_ref.dtype)


def _moba_prompt(qf, q16, k16, v16_t, kmean_p, g_col):
    t, w = q16.shape
    tq = MOBA_BLOCK
    nbp = kmean_p.shape[0]
    assert nbp == LANES and t // tq <= HEAD_DIM
    assert (t // tq) % 2 == 0
    kc = MOBA_KEY_CHUNK
    while (t // tq) % (2 * kc):
        kc //= 2
    return pl.pallas_call(
        functools.partial(_moba_prompt_kernel, tq=tq, nbp=nbp, kc=kc),
        grid=(w // LANES, t // tq),
        in_specs=[pl.BlockSpec((tq, LANES), lambda p, i: (i, p)),
                  pl.BlockSpec((tq, LANES), lambda p, i: (i, p)),
                  pl.BlockSpec((t, LANES), lambda p, i: (0, p)),
                  pl.BlockSpec((LANES, t), lambda p, i: (p, 0)),
                  pl.BlockSpec((nbp, LANES), lambda p, i: (0, p)),
                  pl.BlockSpec((LANES, 1), lambda p, i: (p, 0))],
        out_specs=pl.BlockSpec((tq, LANES), lambda p, i: (i, p)),
        out_shape=jax.ShapeDtypeStruct((t, w), BF16),
        scratch_shapes=[pltpu.VMEM((2, t, LANES), BF16),
                        pltpu.VMEM((2, kc * tq, tq), F32), pltpu.VMEM((2, kc * tq, tq), F32)],
        compiler_params=_params("arbitrary", "arbitrary"),
        name="moba_prompt_attn",
    )(qf, q16, k16, v16_t, kmean_p, g_col)


def _outproj_kernel(osb_ref, omb_ref, wa_ref, wb_ref, x_ref, ga_ref, g_ref, sc_ref, sh_ref,
                    xo_ref, h_ref):
    mix = _dot(osb_ref[...], wa_ref[...]) + _dot(omb_ref[...], wb_ref[...])
    x = x_ref[...] + ga_ref[...] * mix
    xo_ref[...] = x
    h_ref[...] = (_rms(x) * g_ref[...] * (1.0 + sc_ref[...]) + sh_ref[...]).astype(h_ref.dtype)


def _outproj(o_sb, o_mb, w_a, w_b, x, ga, g, sc, sh, tm):
    t, d = x.shape
    w = o_sb.shape[1]
    row = lambda width: pl.BlockSpec((tm, width), lambda i: (i, 0))
    mod = lambda a: _row_spec(a.shape[0], tm, d)
    return pl.pallas_call(
        _outproj_kernel,
        grid=(t // tm,),
        in_specs=[row(w), row(w), _const_spec(w_a.shape), _const_spec(w_b.shape), row(d),
                  mod(ga), _const_spec((1, d)), mod(sc), mod(sh)],
        out_specs=[row(d), row(d)],
        out_shape=[jax.ShapeDtypeStruct((t, d), F32), jax.ShapeDtypeStruct((t, d), BF16)],
        compiler_params=_params("arbitrary"),
        name="out_proj",
    )(o_sb, o_mb, w_a, w_b, x, ga, g, sc, sh)


FFN_CHUNK = 256


def _silu_gate(val, gate):
    return val * gate * (1.0 / (1.0 + jnp.exp(-gate)))


def _ffn_prompt_kernel(h_ref, wup_ref, cw_ref, cb_ref, wdn_ref, x_ref, ga_ref, prev_ref, gfin_ref,
                       xo_ref, state_ref, carry_ref, *, d_ff, tm, final):
    i = pl.program_id(0)

    @pl.when(i == 0)
    def _():
        carry_ref[...] = prev_ref[...]

    h = h_ref[...]
    row = lax.broadcasted_iota(jnp.int32, (tm, FFN_CHUNK), 0)

    def conv(off):
        cols = pl.ds(off, FFN_CHUNK)
        u = _dot(h, wup_ref[:, cols])
        prev = carry_ref[:, cols]
        p1, p2 = prev[SUBLANES - 1:SUBLANES, :], prev[SUBLANES - 2:SUBLANES - 1, :]
        u1 = jnp.where(row == 0, p1, pltpu.roll(u, 1, 0))
        u2 = jnp.where(row == 0, p2, jnp.where(row == 1, p1, pltpu.roll(u, 2, 0)))
        carry_ref[:, cols] = u[tm - SUBLANES:, :]
        cw = cw_ref[:, cols]
        return cb_ref[:, cols] + u2 * cw[0:1, :] + u1 * cw[1:2, :] + u * cw[2:3, :]

    acc = jnp.zeros((tm, xo_ref.shape[1]), F32)
    for c in range(d_ff // FFN_CHUNK):
        off = c * FFN_CHUNK
        act = _silu_gate(conv(off), conv(d_ff + off))
        acc = acc + _dot(act.astype(BF16), wdn_ref[pl.ds(off, FFN_CHUNK), :])
    x_new = x_ref[...] + ga_ref[...] * acc
    xo_ref[...] = _rms(x_new) * gfin_ref[...] if final else x_new

    @pl.when(i == pl.num_programs(0) - 1)
    def _():
        state_ref[...] = carry_ref[...]


def _ffn_prompt(h, w_up, conv_w, conv_b, w_down, x, ga, prev8, g_fin, final, tm):
    t, d = x.shape
    d_ff = w_down.shape[0]
    row = lambda width: pl.BlockSpec((tm, width), lambda i: (i, 0))
    return pl.pallas_call(
        functools.partial(_ffn_prompt_kernel, d_ff=d_ff, tm=tm, final=final),
        grid=(t // tm,),
        in_specs=[row(d), _const_spec(w_up.shape), _const_spec(conv_w.shape),
                  _const_spec(conv_b.shape), _const_spec(w_down.shape), row(d),
                  _row_spec(ga.shape[0], tm, d), _const_spec(prev8.shape), _const_spec(g_fin.shape)],
        out_specs=[row(d), pl.BlockSpec(prev8.shape, lambda i: (0, 0))],
        out_shape=[jax.ShapeDtypeStruct((t, d), F32), jax.ShapeDtypeStruct(prev8.shape, F32)],
        scratch_shapes=[pltpu.VMEM(prev8.shape, F32)],
        compiler_params=_params("arbitrary"),
        name="conv_ffn_prompt",
    )(h, w_up, conv_w, conv_b, w_down, x, ga, prev8, g_fin)


def _ffn_decode_kernel(h_ref, wup_ref, cw_ref, cb_ref, wdn_ref, x_ref, ga_ref, p0_ref, p1_ref,
                       xo_ref, u_ref, *, d_ff):
    h = h_ref[...]

    def conv(off):
        cols = pl.ds(off, FFN_CHUNK)
        u = _dot(h, wup_ref[:, cols])
        u_ref[:, cols] = u
        cw = cw_ref[:, cols]
        return (cb_ref[:, cols] + p0_ref[:, cols] * cw[0:1, :] + p1_ref[:, cols] * cw[1:2, :]
                + u * cw[2:3, :])

    acc = jnp.zeros(xo_ref.shape, F32)
    for c in range(d_ff // FFN_CHUNK):
        off = c * FFN_CHUNK
        act = _silu_gate(conv(off), conv(d_ff + off))
        acc = acc + _dot(act.astype(BF16), wdn_ref[pl.ds(off, FFN_CHUNK), :])
    xo_ref[...] = x_ref[...] + ga_ref[...] * acc


def _ffn_decode(h, w_up, conv_w, conv_b, w_down, x, ga, prev0, prev1):
    b, d = x.shape
    d_ff = w_down.shape[0]
    full = lambda a: pl.BlockSpec(a.shape, lambda i: (0,) * a.ndim)
    args = (h, w_up, conv_w, conv_b, w_down, x, ga, prev0, prev1)
    return pl.pallas_call(
        functools.partial(_ffn_decode_kernel, d_ff=d_ff),
        grid=(1,),
        in_specs=[full(a) for a in args],
        out_specs=[pl.BlockSpec((b, d), lambda i: (0, 0)),
                   pl.BlockSpec((b, 2 * d_ff), lambda i: (0, 0))],
        out_shape=[jax.ShapeDtypeStruct((b, d), F32), jax.ShapeDtypeStruct((b, 2 * d_ff), F32)],
        compiler_params=_params("arbitrary"),
        name="conv_ffn_decode",
    )(*args)


def _final_norm_kernel(x_ref, g_ref, o_ref):
    o_ref[...] = _rms(x_ref[...]) * g_ref[...]


def _final_norm(x, g, tm):
    t, d = x.shape
    return pl.pallas_call(
        _final_norm_kernel,
        grid=(t // tm,),
        in_specs=[pl.BlockSpec((tm, d), lambda i: (i, 0)), pl.BlockSpec((1, d), lambda i: (0, 0))],
        out_specs=pl.BlockSpec((tm, d), lambda i: (i, 0)),
        out_shape=jax.ShapeDtypeStruct((t, d), F32),
        compiler_params=_params("arbitrary"),
        name="final_norm",
    )(x, g)


def _page_scores(qb, k_page, heads):
    sub = lax.broadcasted_iota(jnp.int32, (heads, k_page.shape[-1]), 0)
    z = jnp.zeros(sub.shape, F32)
    for h in range(heads):
        row = jnp.sum(qb[h] * k_page[h], axis=0, keepdims=True)
        z = jnp.where(sub == h, row, z)
    return z


def _sb_decode_kernel(pt_ref, qb_ref, k_hbm, v_hbm, g_ref, o_ref, kbuf, vbuf, acc_ref, sem,
                      *, layer, n_pages, group, heads):
    b = pl.program_id(0)
    n_chunks = n_pages // group
    page = kbuf.shape[-1]
    lane = lax.broadcasted_iota(jnp.int32, (heads, page), 1)
    qb = qb_ref.at[0]

    def copies(c, slot):
        out = []
        for g in range(group):
            pg = pt_ref[b, c * group + g]
            out.append(pltpu.make_async_copy(k_hbm.at[layer, pg], kbuf.at[slot, g], sem.at[0, slot]))
            out.append(pltpu.make_async_copy(v_hbm.at[layer, pg], vbuf.at[slot, g], sem.at[1, slot]))
        return out

    def start(c, slot):
        for cp in copies(c, slot):
            cp.start()

    def wait(c, slot):
        for cp in copies(c, slot):
            cp.wait()

    def page_update(slot, g, keep):
        z = _page_scores(qb, kbuf.at[slot, g], heads) * HEAD_DIM ** -0.5
        sp = _softplus(z)
        suffix = sp
        shift = 1
        while shift < page:
            moved = pltpu.roll(suffix, page - shift, 1)
            suffix = suffix + jnp.where(lane + shift < page, moved, 0.0)
            shift *= 2
        wgt = jnp.exp(z - sp - (suffix - sp) - keep)
        for h in range(heads):
            acc_ref[h] += wgt[h:h + 1, :] * vbuf[slot, g, h]
        return keep + jnp.sum(sp, axis=1, keepdims=True)

    def step(carry):
        c, _, keep = carry
        slot = c % 2

        @pl.when(c > 0)
        def _():
            start(c - 1, 1 - slot)

        wait(c, slot)
        for g in reversed(range(group)):
            keep = page_update(slot, g, keep)
        return c - 1, jnp.min(keep) < SB_EXIT, keep

    acc_ref[...] = jnp.zeros_like(acc_ref)
    last = n_chunks - 1
    start(last, last % 2)
    c_end, _, _ = lax.while_loop(lambda cr: jnp.logical_and(cr[0] >= 0, cr[1]), step,
                                 (jnp.int32(last), jnp.bool_(True), jnp.zeros((heads, 1), F32)))

    @pl.when(c_end >= 0)
    def _():
        wait(c_end, c_end % 2)

    for h in range(heads):
        o = jnp.sum(acc_ref[h], axis=1, keepdims=True)
        o_ref[0, h] = o * lax.rsqrt(jnp.mean(o * o, axis=0, keepdims=True) + RMS_EPS) * g_ref[h]


def _sb_decode(page_table, qb, cache_k, cache_v, layer, g):
    b, heads, hd, page = qb.shape
    n_pages = page_table.shape[1]
    group = DECODE_PAGE_GROUP
    while n_pages % group:
        group //= 2
    grid_spec = pltpu.PrefetchScalarGridSpec(
        num_scalar_prefetch=1,
        grid=(b,),
        in_specs=[pl.BlockSpec((1, heads, hd, page), lambda i, pt: (i, 0, 0, 0)),
                  pl.BlockSpec(memory_space=pl.ANY), pl.BlockSpec(memory_space=pl.ANY),
                  pl.BlockSpec((heads, hd, 1), lambda i, pt: (0, 0, 0))],
        out_specs=pl.BlockSpec((1, heads, hd, 1), lambda i, pt: (i, 0, 0, 0)),
        scratch_shapes=[pltpu.VMEM((2, group, heads, hd, page), F32),
                        pltpu.VMEM((2, group, heads, hd, page), F32),
                        pltpu.VMEM((heads, hd, page), F32),
                        pltpu.SemaphoreType.DMA((2, 2))],
    )
    return pl.pallas_call(
        functools.partial(_sb_decode_kernel, layer=layer, n_pages=n_pages, group=group, heads=heads),
        grid_spec=grid_spec,
        out_shape=jax.ShapeDtypeStruct((b, heads, hd, 1), F32),
        compiler_params=_params("arbitrary"),
        name="sb_decode_attn",
    )(page_table, qb, cache_k, cache_v, g)


def _moba_gate_kernel(pt_ref, qb_ref, *refs, group, ppb, heads):
    del pt_ref
    k_refs = refs[:group]
    sel_ref, gate_ref = refs[group], refs[group + 1]
    s = pl.program_id(1)
    page = k_refs[0].shape[-1]
    lane = lax.broadcasted_iota(jnp.int32, (heads, page), 1)
    qb = qb_ref.at[0]

    @pl.when(s == 0)
    def _():
        gate_ref[...] = jnp.full(gate_ref.shape, -jnp.inf, F32)

    gates = gate_ref[...]
    for blk in range(group // ppb):
        z = _page_scores(qb, k_refs[blk * ppb], heads)
        for o in range(1, ppb):
            z = z + _page_scores(qb, k_refs[blk * ppb + o], heads)
        gate = jnp.sum(z, axis=1, keepdims=True) * (1.0 / MOBA_BLOCK)
        gates = jnp.where(lane == s * (group // ppb) + blk, gate, gates)
    gate_ref[...] = gates

    @pl.when(s == pl.num_programs(1) - 1)
    def _():
        gates = gate_ref[...]
        for r in range(MOBA_TOPK):
            best = jnp.max(gates, axis=1, keepdims=True)
            idx = jnp.min(jnp.where(gates == best, lane, page), axis=1, keepdims=True)
            sel_ref[0, r] = jnp.broadcast_to(idx, (heads, page))
            gates = jnp.where(lane == idx, -jnp.inf, gates)


def _moba_gate(page_table, qb, cache_k, layer):
    b, heads, hd, page = qb.shape
    n_pages = page_table.shape[1]
    ppb = MOBA_BLOCK // page
    assert n_pages // ppb <= page
    group = max(ppb, GATE_PAGE_GROUP)
    while n_pages % group:
        group -= ppb
    cache_specs = [pl.BlockSpec((None, None, heads, hd, page),
                                functools.partial(lambda i, s, pt, o: (layer, pt[i, group * s + o], 0, 0, 0), o=o))
                   for o in range(group)]
    grid_spec = pltpu.PrefetchScalarGridSpec(
        num_scalar_prefetch=1,
        grid=(b, n_pages // group),
        in_specs=[pl.BlockSpec((1, heads, hd, page), lambda i, s, pt: (i, 0, 0, 0))] + cache_specs,
        out_specs=pl.BlockSpec((1, MOBA_TOPK, heads, page), lambda i, s, pt: (i, 0, 0, 0)),
        scratch_shapes=[pltpu.VMEM((heads, page), F32)],
    )
    sel = pl.pallas_call(
        functools.partial(_moba_gate_kernel, group=group, ppb=ppb, heads=heads),
        grid_spec=grid_spec,
        out_shape=jax.ShapeDtypeStruct((b, MOBA_TOPK, heads, page), jnp.int32),
        compiler_params=_params("arbitrary", "arbitrary"),
        name="moba_decode_gate",
    )(page_table, qb, *([cache_k] * group))
    return sel[..., 0]


def _moba_decode_kernel(pt_ref, sel_ref, qb_ref, kn_ref, vn_ref, *refs, n_sel, hps):
    del pt_ref, sel_ref
    n_pages = n_sel * hps
    k_refs, v_refs = refs[:n_pages], refs[n_pages:2 * n_pages]
    g_ref, o_ref = refs[2 * n_pages], refs[2 * n_pages + 1]
    scale = HEAD_DIM ** -0.5
    for hh in range(hps):
        qb = qb_ref[hh]
        ks, vs = k_refs[hh * n_sel:(hh + 1) * n_sel], v_refs[hh * n_sel:(hh + 1) * n_sel]
        scores = [jnp.sum(qb * r[...], axis=0, keepdims=True) * scale for r in ks]
        s_own = jnp.sum(qb[:, :1] * kn_ref[hh], axis=0, keepdims=True) * scale
        m = s_own
        for s in scores:
            m = jnp.maximum(m, jnp.max(s, axis=1, keepdims=True))
        p_own = jnp.exp(s_own - m)
        l = p_own
        acc = jnp.zeros(qb.shape, F32)
        for s, v_ref in zip(scores, vs):
            p = jnp.exp(s - m)
            l = l + jnp.sum(p, axis=1, keepdims=True)
            acc = acc + p * v_ref[...]
        o = (jnp.sum(acc, axis=1, keepdims=True) + p_own * vn_ref[hh]) / l
        o_ref[hh] = o * lax.rsqrt(jnp.mean(o * o, axis=0, keepdims=True) + RMS_EPS) * g_ref[hh]


def _moba_decode(page_table, sel_flat, qb, k_new, v_new, cache_k, cache_v, layer, g):
    b, heads, hd, page = qb.shape
    ppb = MOBA_BLOCK // page
    n_sel = MOBA_TOPK * ppb
    hps = MOBA_DECODE_HEADS
    while heads % hps:
        hps //= 2

    def cache_map(i, hp, pt, sel, hh, n):
        h = hp * hps + hh
        blk = sel[(i * MOBA_TOPK + n // ppb) * heads + h]
        return (layer, pt[i, blk * ppb + n % ppb], h, 0, 0)

    cache_specs = [pl.BlockSpec((None, None, None, hd, page), functools.partial(cache_map, hh=hh, n=n))
                   for hh in range(hps) for n in range(n_sel)]
    head_spec = lambda last: pl.BlockSpec((None, hps, hd, last), lambda i, hp, pt, sel: (i, hp, 0, 0))
    grid_spec = pltpu.PrefetchScalarGridSpec(
        num_scalar_prefetch=2,
        grid=(b, heads // hps),
        in_specs=[head_spec(page), head_spec(1), head_spec(1)] + cache_specs + cache_specs
                 + [pl.BlockSpec((hps, hd, 1), lambda i, hp, pt, sel: (hp, 0, 0))],
        out_specs=head_spec(1),
    )
    n_pages = n_sel * hps
    return pl.pallas_call(
        functools.partial(_moba_decode_kernel, n_sel=n_sel, hps=hps),
        grid_spec=grid_spec,
        out_shape=jax.ShapeDtypeStruct((b, heads, hd, 1), F32),
        compiler_params=_params("arbitrary", "arbitrary"),
        name="moba_decode_attn",
    )(page_table, sel_flat, qb, k_new, v_new, *([cache_k] * n_pages), *([cache_v] * n_pages), g)


def _rope_tables(pos):
    half = HEAD_DIM // 2
    inv_freq = ROPE_THETA ** (-jnp.arange(half, dtype=F32) / half)
    ang = pos.astype(F32)[:, None] * inv_freq[None, :]
    cos, sin = jnp.cos(ang), jnp.sin(ang)
    zero = jnp.zeros_like(sin)
    reps = LANES // HEAD_DIM
    cos_t = jnp.tile(jnp.concatenate([cos, cos], axis=1), (1, reps))
    sin_a = jnp.tile(jnp.concatenate([-sin, zero], axis=1), (1, reps))
    sin_b = jnp.tile(jnp.concatenate([zero, sin], axis=1), (1, reps))
    return cos_t, sin_a, sin_b


def _row_tile(t, target):
    tm = min(t, target)
    while t % tm:
        tm //= 2
    return tm


def kernel(x_prompt, x_sample, c_prompt, c_sample, cache_sb_k, cache_sb_v, cache_moba_k, cache_moba_v, state_ffn_conv, page_table, w_ada, b_ada, g_attn, w_in, g_sb_out, g_moba_out, w_out, g_ffn, w_up, conv_w, conv_b, w_down, g_final):
    batch, seq, d = x_prompt.shape
    dec_b, dec_seq, _ = x_sample.shape
    depth = w_ada.shape[0]
    n_pages, page = page_table.shape[1], cache_sb_k.shape[2]
    heads = cache_sb_k.shape[3]
    w = heads * HEAD_DIM
    d_ff = w_down.shape[1]
    past_len = n_pages * page
    assert dec_seq == 1 and d == 2 * w and w_in.shape[2] == 6 * w
    assert seq % MOBA_BLOCK == 0 and past_len % MOBA_BLOCK == 0 and MOBA_BLOCK % page == 0
    assert past_len // MOBA_BLOCK >= MOBA_TOPK and d_ff % FFN_CHUNK == 0

    n_c = batch + dec_b
    c_all = jnp.concatenate([c_prompt, c_sample], axis=0)
    c_all = jnp.pad(c_all, ((0, -n_c % SUBLANES), (0, 0)))
    mod = _ada(c_all, w_ada, b_ada)

    sb_k_t, sb_v_t, mb_k_t, mb_v_t = (jnp.transpose(c, (0, 1, 3, 4, 2)) for c in
                                      (cache_sb_k, cache_sb_v, cache_moba_k, cache_moba_v))
    rope_p = _rope_tables(jnp.arange(seq))
    rope_s = _rope_tables(jnp.full((dec_b,), past_len))
    nb = seq // MOBA_BLOCK
    nbp = -(-nb // LANES) * LANES
    tm_p = _row_tile(seq, 512)
    xs = x_sample.reshape(dec_b, d)
    xps = [x_prompt[b] for b in range(batch)]
    zeros_prev = jnp.zeros((SUBLANES, 2 * d_ff), F32)
    g_fin = g_final.reshape(1, d)

    rows_p = [[] for _ in range(4)]
    rows_s = [[] for _ in range(4)]
    conv_p, conv_s = [], []
    for l in range(depth):
        wl = w_in[l]
        wqk = jnp.concatenate([wl[:, 3 * w:4 * w], wl[:, 4 * w:5 * w]], axis=1)
        wqk_hi = wqk.astype(BF16)
        wqk_lo = (wqk - wqk_hi.astype(F32)).astype(BF16)
        wts = (wl[:, :3 * w].astype(BF16), wqk_hi, wqk_lo, wl[:, 5 * w:].astype(BF16))
        wo_a, wo_b = w_out[l, :w].astype(BF16), w_out[l, w:].astype(BF16)
        wup, wdn = w_up[l].astype(BF16), w_down[l].astype(BF16)
        cw, cb = conv_w[l], conv_b[l].reshape(1, -1)
        g_a, g_f = g_attn[l].reshape(1, d), g_ffn[l].reshape(1, d)
        g_sb, g_mb = g_sb_out[l].reshape(1, w), g_moba_out[l].reshape(1, w)

        def mods(lo, hi):
            m = mod[l, lo:hi]
            return [m[:, k * d:(k + 1) * d] for k in range(6)]

        layer_rows = [[] for _ in range(4)]
        layer_conv = []
        for b in range(batch):
            sh1, sc1, ga1, sh2, sc2, ga2 = mods(b, b + 1)
            (k_sb, v_sb, k_mb, v_mb, q_mbf, q_sb16, k_sb16, v_sb16, q_mb16, k_mb16,
             v_mb16) = _proj(xps[b], sc1, sh1, g_a, wts, rope_p, tm_p)
            o_sb = _sb_prompt(q_sb16, k_sb16, v_sb16, g_sb, MOBA_BLOCK)
            kmean = jnp.pad(_kmean(k_mb), ((0, nbp - nb), (0, 0)))
            o_mb = _moba_prompt(q_mbf, q_mb16, k_mb16, v_mb16.T, kmean, g_mb.reshape(w, 1))
            x_mid, h2 = _outproj(o_sb, o_mb, wo_a, wo_b, xps[b], ga1, g_f, sc2, sh2, tm_p)
            xps[b], state8 = _ffn_prompt(h2, wup, cw, cb, wdn, x_mid, ga2, zeros_prev, g_fin,
                                         l == depth - 1, tm_p)
            for lst, a in zip(layer_rows, (k_sb, v_sb, k_mb, v_mb)):
                lst.append(a.reshape(seq, heads, HEAD_DIM))
            layer_conv.append(state8[SUBLANES - (CONV_W - 1):])
        for lst, parts in zip(rows_p, layer_rows):
            lst.append(jnp.stack(parts))
        conv_p.append(jnp.stack(layer_conv))

        sh1, sc1, ga1, sh2, sc2, ga2 = mods(batch, n_c)
        (k_sb, v_sb, k_mb, v_mb, q_mbf, q_sb16, _, _, _, _, _) = _proj(
            xs, sc1, sh1, g_a, wts, rope_s, dec_b)
        col = lambda a: a.reshape(dec_b, heads, HEAD_DIM, 1)
        lanes = lambda a: jnp.broadcast_to(col(a), (dec_b, heads, HEAD_DIM, page))
        q_sb = q_sb16.astype(F32) * HEAD_DIM ** 0.5
        o_sb = _sb_decode(page_table, lanes(q_sb), sb_k_t, sb_v_t, l, g_sb.reshape(heads, HEAD_DIM, 1))
        qb_mb = lanes(q_mbf)
        sel = _moba_gate(page_table, qb_mb, mb_k_t, l)
        o_mb = _moba_decode(page_table, sel.reshape(-1), qb_mb, col(k_mb), col(v_mb),
                            mb_k_t, mb_v_t, l, g_mb.reshape(heads, HEAD_DIM, 1))
        x_mid, h2 = _outproj(o_sb.reshape(dec_b, w).astype(BF16), o_mb.reshape(dec_b, w).astype(BF16),
                             wo_a, wo_b, xs, ga1, g_f, sc2, sh2, dec_b)
        prev = state_ffn_conv[l]
        xs, u_new = _ffn_decode(h2, wup, cw, cb, wdn, x_mid, ga2, prev[:, 0], prev[:, 1])
        for lst, a in zip(rows_s, (k_sb, v_sb, k_mb, v_mb)):
            lst.append(a.reshape(dec_b, 1, heads, HEAD_DIM))
        conv_s.append(jnp.stack([prev[:, 1], u_new], axis=1))

    y_prompt = jnp.stack(xps)
    y_sample = _final_norm(xs, g_fin, dec_b).reshape(dec_b, 1, d)
    return (y_prompt, y_sample,
            jnp.stack(rows_p[0]), jnp.stack(rows_p[1]), jnp.stack(rows_p[2]), jnp.stack(rows_p[3]),
            jnp.stack(conv_p),
            jnp.stack(rows_s[0]), jnp.stack(rows_s[1]), jnp.stack(rows_s[2]), jnp.stack(rows_s[3]),
            jnp.stack(conv_s))
```
